```python
import math
import jax, jax.numpy as jnp
from jax import lax
import numpy as np


D_MODEL = 4096
BATCH = 1
SEQ = 8192
DEPTH = 2

CHUNK = 64
N_A_LAYERS = DEPTH // 2
N_B_LAYERS = DEPTH - N_A_LAYERS
POOL_EXPAND = 2
POOL_WIDTH = POOL_EXPAND * D_MODEL
POOL_WINDOWS = (2, 4, 8, 16)
N_POOL_GROUPS = len(POOL_WINDOWS)
POOL_GROUP = POOL_WIDTH // N_POOL_GROUPS
DIFF_HEAD_DIM = 128
N_DIFF_HEADS = D_MODEL // (2 * DIFF_HEAD_DIM)
ATT_WIDTH = N_DIFF_HEADS * 2 * DIFF_HEAD_DIM
ROPE_THETA = 10000.0
Q_BLOCK = 128
EPS = 1e-6

kernel_name = "yoco_pool_diffattn_hybrid"


def rms_norm(x, g):
    xf = x.astype(jnp.float32)
    y = xf * lax.rsqrt(jnp.mean(xf * xf, axis=-1, keepdims=True) + EPS)
    return (y * g.astype(jnp.float32)).astype(x.dtype)


def ada_mod(c, w, b, n):
    m = jnp.einsum('bd,de->be', jax.nn.silu(c), w) + b
    return jnp.split(m[:, None, :], n, axis=-1)


def rotary(x, positions):
    dh = x.shape[-1]
    half = dh // 2
    inv = ROPE_THETA ** (-jnp.arange(0, dh, 2, dtype=jnp.float32) / dh)
    ang = positions.astype(jnp.float32)[..., None] * inv
    cos = jnp.cos(ang)[:, :, None, :]
    sin = jnp.sin(ang)[:, :, None, :]
    xf = x.astype(jnp.float32)
    x1, x2 = xf[..., :half], xf[..., half:]
    out = jnp.concatenate([x1 * cos - x2 * sin, x2 * cos + x1 * sin], axis=-1)
    return out.astype(x.dtype)


def pool_mixer(u, w_group, scale):
    B, S, _ = u.shape
    t = jnp.arange(S)
    ug = u.astype(jnp.float32).reshape(B, S, N_POOL_GROUPS, POOL_GROUP)
    cs = jnp.cumsum(ug, axis=1)
    outs = []
    for g, w in enumerate(POOL_WINDOWS):
        csg = cs[:, :, g]
        prev = jnp.pad(csg, ((0, 0), (w, 0), (0, 0)))[:, :S]
        cnt = jnp.minimum(t + 1, w).astype(jnp.float32)[None, :, None]
        outs.append((csg - prev) / cnt - ug[:, :, g])
    pooled = jnp.stack(outs, axis=2).astype(u.dtype)
    mixed = jnp.einsum('bsgc,gce->bsge', pooled, w_group)
    return mixed.reshape(B, S, POOL_WIDTH) * scale


def diff_attention(q, k, v, lam):
    B, S, H, _, dh = q.shape
    nb = S // Q_BLOCK
    qb = q.reshape(B, nb, Q_BLOCK, H, 2, dh).transpose(1, 0, 3, 4, 2, 5)
    kt = k.transpose(0, 2, 3, 1, 4)
    vt = v.transpose(0, 2, 1, 3)
    key_chunk = jnp.arange(S) // CHUNK
    sm_scale = dh ** -0.5

    def block(args):
        q_blk, i = args
        s = jnp.einsum('bhcqd,bhckd->bhcqk', q_blk, kt).astype(jnp.float32) * sm_scale
        q_chunk = (i * Q_BLOCK + jnp.arange(Q_BLOCK)) // CHUNK
        mask = key_chunk[None, :] <= q_chunk[:, None]
        s = jnp.where(mask, s, -jnp.inf)
        p = jax.nn.softmax(s, axis=-1)
        pd = p[:, :, 0] - lam * p[:, :, 1]
        return jnp.einsum('bhqk,bhkd->bhqd', pd.astype(v.dtype), vt)

    o = lax.map(block, (qb, jnp.arange(nb)))
    return o.transpose(1, 0, 3, 2, 4).reshape(B, S, H, 2 * dh)


def setup_inputs(seed: int = 0) -> dict:
    key = jax.random.key(seed)
    ks = jax.random.split(key, 24)
    D = D_MODEL
    f32 = jnp.float32

    def nrm(k, shape, fan_in, mult=1.0):
        return jax.random.normal(k, shape, f32) * (mult * fan_in ** -0.5)

    def gain(k, shape):
        return 1.0 + 0.02 * jax.random.normal(k, shape, f32)

    def bias(k, shape):
        return 0.02 * jax.random.normal(k, shape, f32)

    x = jax.random.normal(ks[0], (BATCH, SEQ, D), f32)
    c = jax.random.normal(ks[1], (BATCH, D), f32)
    positions = jnp.broadcast_to(jnp.arange(SEQ, dtype=jnp.int32)[None, :], (BATCH, SEQ))
    nA, nB = N_A_LAYERS, N_B_LAYERS
    return {
        'x': x,
        'c': c,
        'positions': positions,
        'a_norm_g': gain(ks[2], (nA, D)),
        'a_ada_w': nrm(ks[3], (nA, D, 3 * D), D, 0.5),
        'a_ada_b': bias(ks[4], (nA, 3 * D)),
        'a_w_in': nrm(ks[5], (nA, D, 2 * POOL_WIDTH), D),
        'a_w_group': nrm(ks[6], (nA, N_POOL_GROUPS, POOL_GROUP, POOL_GROUP), POOL_GROUP),
        'a_scale': gain(ks[7], (nA, POOL_WIDTH)),
        'a_w_out': nrm(ks[8], (nA, POOL_WIDTH, D), POOL_WIDTH),
        'kv_norm_g': gain(ks[9], (D,)),
        'kv_ada_w': nrm(ks[10], (D, 2 * D), D, 0.5),
        'kv_ada_b': bias(ks[11], (2 * D,)),
        'w_kv': nrm(ks[12], (D, 2 * ATT_WIDTH), D),
        'b_norm_g': gain(ks[13], (nB, D)),
        'b_ada_w': nrm(ks[14], (nB, D, 3 * D), D, 0.5),
        'b_ada_b': bias(ks[15], (nB, 3 * D)),
        'b_w_qg': nrm(ks[16], (nB, D, 2 * ATT_WIDTH), D),
        'b_lam': 0.1 * jax.random.normal(ks[17], (nB, 4, DIFF_HEAD_DIM), f32),
        'b_subln_g': gain(ks[18], (nB, 2 * DIFF_HEAD_DIM)),
        'b_w_out': nrm(ks[19], (nB, ATT_WIDTH, D), ATT_WIDTH),
        'final_g': gain(ks[20], (D,)),
    }


def reference(x, c, positions, a_norm_g, a_ada_w, a_ada_b, a_w_in, a_w_group, a_scale, a_w_out,
              kv_norm_g, kv_ada_w, kv_ada_b, w_kv,
              b_norm_g, b_ada_w, b_ada_b, b_w_qg, b_lam, b_subln_g, b_w_out, final_g):
    B, S, _ = x.shape
    H, dh = N_DIFF_HEADS, DIFF_HEAD_DIM
    h = x
    k_shared = None
    v_shared = None
    for l in range(DEPTH):
        if l < N_A_LAYERS:
            i = l
            shift, scale, gate = ada_mod(c, a_ada_w[i], a_ada_b[i], 3)
            hn = rms_norm(h, a_norm_g[i]) * (1 + scale) + shift
            u, z = jnp.split(hn @ a_w_in[i], 2, axis=-1)
            y = pool_mixer(u, a_w_group[i], a_scale[i]) * jax.nn.silu(z)
            h = h + gate * (y @ a_w_out[i])
        else:
            j = l - N_A_LAYERS
            if j == 0:
                kv_shift, kv_scale = ada_mod(c, kv_ada_w, kv_ada_b, 2)
                kvn = rms_norm(h, kv_norm_g) * (1 + kv_scale) + kv_shift
                k_flat, v_flat = jnp.split(kvn @ w_kv, 2, axis=-1)
                k_shared = rotary(k_flat.reshape(B, S, 2 * H, dh), positions).reshape(B, S, H, 2, dh)
                v_shared = v_flat.reshape(B, S, H, 2 * dh)
            shift, scale, gate = ada_mod(c, b_ada_w[j], b_ada_b[j], 3)
            hn = rms_norm(h, b_norm_g[j]) * (1 + scale) + shift
            q_flat, z = jnp.split(hn @ b_w_qg[j], 2, axis=-1)
            q = rotary(q_flat.reshape(B, S, 2 * H, dh), positions).reshape(B, S, H, 2, dh)
            lp = b_lam[j].astype(jnp.float32)
            lam_init = 0.8 - 0.6 * math.exp(-0.3 * l)
            lam = jnp.exp(jnp.sum(lp[0] * lp[1])) - jnp.exp(jnp.sum(lp[2] * lp[3])) + lam_init
            o = diff_attention(q, k_shared, v_shared, lam)
            o = rms_norm(o, b_subln_g[j]) * (1.0 - lam_init)
            y = o.reshape(B, S, ATT_WIDTH) * jax.nn.silu(z)
            h = h + gate * (y @ b_w_out[j])
    return rms_norm(h, final_g)
```

```python
import functools
import math

import jax
import jax.numpy as jnp
from jax import lax
from jax.experimental import pallas as pl
from jax.experimental.pallas import tpu as pltpu

_EPS = 1e-6
_POOL_WINDOWS = (2, 4, 8, 16)
_MAX_WINDOW = max(_POOL_WINDOWS)
_HEAD_DIM = 128
_CHUNK = 64
_ROPE_THETA = 10000.0

_V7X_LANES = 128
_V7X_VMEM_BYTES = 64 * 1024 * 1024
_V7X_VMEM_BUDGET = _V7X_VMEM_BYTES - 8 * 1024 * 1024

_BF16 = jnp.bfloat16
_F32 = jnp.float32


def _params(semantics, vmem_bytes):
    return pltpu.CompilerParams(
        dimension_semantics=semantics,
        vmem_limit_bytes=int(min(max(vmem_bytes, 16 * 1024 * 1024), _V7X_VMEM_BUDGET)),
    )


def _silu(x):
    return x * (1.0 / (1.0 + jnp.exp(-x)))


def _ada_kernel(c_ref, w_ref, b_ref, o_ref):
    sc = _silu(c_ref[...])
    lhs = jnp.broadcast_to(sc, (8, sc.shape[1]))
    acc = jnp.dot(lhs, w_ref[...], preferred_element_type=_F32)
    o_ref[...] = acc[0:1, :] + b_ref[...]


def _ada_mod(c, w, b):
    d, n = w.shape
    tn = 512
    return pl.pallas_call(
        _ada_kernel,
        grid=(n // tn,),
        in_specs=[
            pl.BlockSpec((1, d), lambda j: (0, 0)),
            pl.BlockSpec((d, tn), lambda j: (0, j)),
            pl.BlockSpec((1, tn), lambda j: (0, j)),
        ],
        out_specs=pl.BlockSpec((1, tn), lambda j: (0, j)),
        out_shape=jax.ShapeDtypeStruct((1, n), _F32),
        compiler_params=_params(("arbitrary",), 3 * d * tn * 4),
        name="ada_mod",
    )(c, w, b.reshape(1, n))


def _norm_pool_kernel(x_ref, g_ref, sc_ref, sh_ref, o_ref, carry_ref, *, ts):
    i = pl.program_id(0)

    @pl.when(i == 0)
    def _():
        carry_ref[...] = jnp.zeros_like(carry_ref)

    x = x_ref[...]
    ms = jnp.mean(x * x, axis=-1, keepdims=True)
    hn = (x * lax.rsqrt(ms + _EPS)) * g_ref[...] * (1.0 + sc_ref[...]) + sh_ref[...]
    o_ref[0] = hn.astype(_BF16)

    m = _MAX_WINDOW
    ext = jnp.concatenate([carry_ref[...], hn], axis=0)
    carry_ref[...] = hn[ts - m:, :]

    t = i * ts + lax.broadcasted_iota(jnp.int32, (ts, 1), 0)
    win = ext
    width = 1
    for gidx, w in enumerate(_POOL_WINDOWS):
        while width < w:
            win = win[width:, :] + win[:-width, :]
            width *= 2
        start = m - (width - 1)
        wsum = win[start:start + ts, :]
        inv_cnt = 1.0 / jnp.minimum(t + 1, w).astype(_F32)
        o_ref[1 + gidx] = (wsum * inv_cnt - hn).astype(_BF16)


def _norm_pool(x, g, scale, shift):
    s, d = x.shape
    ts = 128
    n_out = 1 + len(_POOL_WINDOWS)
    vec = pl.BlockSpec((1, d), lambda i: (0, 0))
    return pl.pallas_call(
        functools.partial(_norm_pool_kernel, ts=ts),
        grid=(s // ts,),
        in_specs=[pl.BlockSpec((ts, d), lambda i: (i, 0)), vec, vec, vec],
        out_specs=pl.BlockSpec((n_out, ts, d), lambda i: (0, i, 0)),
        out_shape=jax.ShapeDtypeStruct((n_out, s, d), _BF16),
        scratch_shapes=[pltpu.VMEM((_MAX_WINDOW, d), _F32)],
        compiler_params=_params(("arbitrary",), 2 * ts * d * (4 + 2 * n_out) + 16 * ts * d * 4),
        name="norm_pool",
    )(x, g.reshape(1, d), scale, shift)


def _norm2_kernel(x_ref, g1_ref, sc1_ref, sh1_ref, g2_ref, sc2_ref, sh2_ref, o1_ref, o2_ref):
    x = x_ref[...]
    ms = jnp.mean(x * x, axis=-1, keepdims=True)
    xn = x * lax.rsqrt(ms + _EPS)
    o1_ref[...] = (xn * g1_ref[...] * (1.0 + sc1_ref[...]) + sh1_ref[...]).astype(_BF16)
    o2_ref[...] = (xn * g2_ref[...] * (1.0 + sc2_ref[...]) + sh2_ref[...]).astype(_BF16)


def _norm2(x, g1, sc1, sh1, g2, sc2, sh2):
    s, d = x.shape
    ts = 256
    vec = pl.BlockSpec((1, d), lambda i: (0, 0))
    row = pl.BlockSpec((ts, d), lambda i: (i, 0))
    return pl.pallas_call(
        _norm2_kernel,
        grid=(s // ts,),
        in_specs=[row, vec, vec, vec, vec, vec, vec],
        out_specs=[row, row],
        out_shape=[jax.ShapeDtypeStruct((s, d), _BF16)] * 2,
        compiler_params=_params(("arbitrary",), 2 * ts * d * 8 + 6 * ts * d * 4),
        name="norm2",
    )(x, g1.reshape(1, d), sc1, sh1, g2.reshape(1, d), sc2, sh2)


def _final_norm_kernel(x_ref, g_ref, o_ref):
    x = x_ref[...]
    ms = jnp.mean(x * x, axis=-1, keepdims=True)
    o_ref[...] = x * lax.rsqrt(ms + _EPS) * g_ref[...]


def _final_norm(x, g):
    s, d = x.shape
    ts = 256
    row = pl.BlockSpec((ts, d), lambda i: (i, 0))
    return pl.pallas_call(
        _final_norm_kernel,
        grid=(s // ts,),
        in_specs=[row, pl.BlockSpec((1, d), lambda i: (0, 0))],
        out_specs=row,
        out_shape=jax.ShapeDtypeStruct((s, d), _F32),
        compiler_params=_params(("arbitrary",), 4 * ts * d * 4 + 4 * ts * d * 4),
        name="final_norm",
    )(x, g.reshape(1, d))


def _rope_kernel(pos_ref, cos_ref, sin_ref):
    half = _HEAD_DIM // 2
    lane = lax.broadcasted_iota(jnp.int32, (1, _HEAD_DIM), 1)
    f = jnp.where(lane < half, lane, lane - half).astype(_F32)
    inv = jnp.exp(f * (-2.0 * math.log(_ROPE_THETA) / _HEAD_DIM))
    ang = pos_ref[...].astype(_F32) * inv
    cos_ref[...] = jnp.cos(ang)
    sin_ref[...] = jnp.where(lane < half, -1.0, 1.0) * jnp.sin(ang)


def _rope_tables(positions):
    s = positions.shape[-1]
    ts = 1024
    tab = pl.BlockSpec((ts, _HEAD_DIM), lambda i: (i, 0))
    return pl.pallas_call(
        _rope_kernel,
        grid=(s // ts,),
        in_specs=[pl.BlockSpec((ts, 1), lambda i: (i, 0))],
        out_specs=[tab, tab],
        out_shape=[jax.ShapeDtypeStruct((s, _HEAD_DIM), _F32)] * 2,
        compiler_params=_params(("arbitrary",), 16 * ts * _HEAD_DIM * 4),
        name="rope_tables",
    )(positions.reshape(s, 1))


def _mm_kernel(x_ref, w_ref, *rest, epilogue, n_extra):
    extra = rest[:n_extra]
    o_ref = rest[n_extra]
    wbf_ref = rest[n_extra + 1]

    @pl.when(pl.program_id(1) == 0)
    def _():
        wbf_ref[...] = w_ref[...].astype(_BF16)

    acc = jnp.dot(x_ref[...], wbf_ref[...], preferred_element_type=_F32)
    epilogue(acc, extra, o_ref)


def _ep_cast(acc, extra, o_ref):
    o_ref[...] = acc.astype(o_ref.dtype)


def _ep_silu(acc, extra, o_ref):
    o_ref[...] = _silu(acc).astype(o_ref.dtype)


def _ep_scale_gate(acc, extra, o_ref):
    scale_ref, sz_ref = extra
    o_ref[...] = (acc * scale_ref[...] * sz_ref[...].astype(_F32)).astype(o_ref.dtype)


def _ep_residual(acc, extra, o_ref):
    res_ref, gate_ref = extra
    o_ref[...] = res_ref[...] + gate_ref[...] * acc


def _ep_rotary(acc, extra, o_ref, *, mult):
    cos_ref, sin_ref = extra
    cos = cos_ref[...]
    sin = sin_ref[...]
    for c in range(acc.shape[1] // _HEAD_DIM):
        a = acc[:, c * _HEAD_DIM:(c + 1) * _HEAD_DIM]
        rot = a * cos + pltpu.roll(a, _HEAD_DIM // 2, axis=1) * sin
        if mult != 1.0:
            rot = rot * mult
        o_ref[:, c * _HEAD_DIM:(c + 1) * _HEAD_DIM] = rot.astype(o_ref.dtype)


def _matmul(x, w, *, k, n, x_map, w_map, w_block, x_block, out_dtype, epilogue,
            extra=(), extra_specs=(), tm, tn, name):
    m = x.shape[-2]
    out_bytes = jnp.dtype(out_dtype).itemsize
    vmem = (2 * tm * k * 2 + 2 * k * tn * 4 + k * tn * 2 + 2 * tm * tn * out_bytes
            + 3 * tm * tn * 4 + sum(2 * tm * tn * 4 for _ in extra))
    return pl.pallas_call(
        functools.partial(_mm_kernel, epilogue=epilogue, n_extra=len(extra)),
        grid=(n // tn, m // tm),
        in_specs=[pl.BlockSpec(x_block, x_map), pl.BlockSpec(w_block, w_map), *extra_specs],
        out_specs=pl.BlockSpec((tm, tn), lambda j, i: (i, j)),
        out_shape=jax.ShapeDtypeStruct((m, n), out_dtype),
        scratch_shapes=[pltpu.VMEM((k, tn), _BF16)],
        compiler_params=_params(("arbitrary", "arbitrary"), vmem),
        name=name,
    )(x, w, *extra)


def _attn_kernel(q_ref, k_ref, v_ref, sz_ref, lam_ref, g_ref, o_ref, m_ref, l_ref, acc_ref,
                 *, tq, lam_init):
    i = pl.program_id(1)
    dh = _HEAD_DIM
    m_ref[...] = jnp.full_like(m_ref, -jnp.inf)
    l_ref[...] = jnp.zeros_like(l_ref)
    acc_ref[...] = jnp.zeros_like(acc_ref)

    row_chunk = lax.broadcasted_iota(jnp.int32, (tq, tq), 0) // _CHUNK
    col_chunk = lax.broadcasted_iota(jnp.int32, (tq, tq), 1) // _CHUNK
    diag_mask = col_chunk <= row_chunk

    def step(j, masked):
        start = pl.multiple_of(j * tq, tq)
        v_j = v_ref[pl.ds(start, tq), :]
        for c in range(2):
            q_c = q_ref[:, c * dh:(c + 1) * dh]
            k_c = k_ref[pl.ds(start, tq), c * dh:(c + 1) * dh]
            s = lax.dot_general(q_c, k_c, (((1,), (1,)), ((), ())),
                                preferred_element_type=_F32)
            if masked:
                s = jnp.where(diag_mask, s, -jnp.inf)
            m_prev = m_ref[c]
            m_new = jnp.maximum(m_prev, jnp.max(s, axis=1, keepdims=True))
            alpha = jnp.exp(m_prev - m_new)
            p = jnp.exp(s - m_new[:, 0:1])
            l_ref[c] = alpha * l_ref[c] + jnp.sum(p, axis=1, keepdims=True)
            pv = jnp.dot(p.astype(_BF16), v_j, preferred_element_type=_F32)
            acc_ref[c] = jnp.concatenate([alpha, alpha], axis=1) * acc_ref[c] + pv
            m_ref[c] = m_new

    def body(j, carry):
        step(j, False)
        return carry

    lax.fori_loop(0, i, body, 0)
    step(i, True)

    lp = lam_ref[...]
    lam = (jnp.exp(jnp.sum(lp[0:1] * lp[1:2], axis=1, keepdims=True))
           - jnp.exp(jnp.sum(lp[2:3] * lp[3:4], axis=1, keepdims=True)) + lam_init)
    inv1 = 1.0 / l_ref[0][:, 0:1]
    inv2 = 1.0 / l_ref[1][:, 0:1]
    o = acc_ref[0] * inv1 - lam * (acc_ref[1] * inv2)
    ms = jnp.mean(o * o, axis=-1, keepdims=True)
    on = o * lax.rsqrt(ms + _EPS) * g_ref[...] * (1.0 - lam_init)
    o_ref[...] = (on * sz_ref[...].astype(_F32)).astype(o_ref.dtype)


def _attention(q, k, v, sz, lam_params, subln_g, *, lam_init):
    s, width = q.shape
    hw = 2 * _HEAD_DIM
    n_heads = width // hw
    tq = 512
    tile = pl.BlockSpec((tq, hw), lambda h, i: (i, h))
    head = pl.BlockSpec((s, hw), lambda h, i: (0, h))
    vmem = (2 * 2 * s * hw * 2 + 3 * 2 * tq * hw * 2 + 4 * tq * _V7X_LANES * 4
            + 2 * tq * hw * 4 + 8 * tq * tq * 4)
    return pl.pallas_call(
        functools.partial(_attn_kernel, tq=tq, lam_init=lam_init),
        grid=(n_heads, s // tq),
        in_specs=[tile, head, head, tile,
                  pl.BlockSpec((4, _HEAD_DIM), lambda h, i: (0, 0)),
                  pl.BlockSpec((1, hw), lambda h, i: (0, 0))],
        out_specs=tile,
        out_shape=jax.ShapeDtypeStruct((s, width), _BF16),
        scratch_shapes=[pltpu.VMEM((2, tq, _V7X_LANES), _F32),
                        pltpu.VMEM((2, tq, _V7X_LANES), _F32),
                        pltpu.VMEM((2, tq, hw), _F32)],
        compiler_params=_params(("arbitrary", "arbitrary"), vmem),
        name="diff_attention",
    )(q, k, v, sz, lam_params, subln_g.reshape(1, hw))


def kernel(x, c, positions, a_norm_g, a_ada_w, a_ada_b, a_w_in, a_w_group, a_scale, a_w_out,
           kv_norm_g, kv_ada_w, kv_ada_b, w_kv,
           b_norm_g, b_ada_w, b_ada_b, b_w_qg, b_lam, b_subln_g, b_w_out, final_g):
    batch, s, d = x.shape
    n_a = a_norm_g.shape[0]
    n_b = b_norm_g.shape[0]
    assert batch == 1 and n_a == 1 and n_b == 1
    pool_width = a_w_group.shape[1] * a_w_group.shape[2]
    group = a_w_group.shape[2]
    att_width = b_w_out.shape[1]
    assert a_w_group.shape[1] == len(_POOL_WINDOWS)

    h = x[0]
    tm, tn = 1024, 512
    col = lambda j, i: (0, j)
    tile = pl.BlockSpec((tm, tn), lambda j, i: (i, j))
    vec = pl.BlockSpec((1, tn), col)
    w2 = lambda off: (lambda j, i: (0, j + off))

    mod = _ada_mod(c, a_ada_w[0], a_ada_b[0])
    shift, scale, gate = mod[:, :d], mod[:, d:2 * d], mod[:, 2 * d:]
    hn5 = _norm_pool(h, a_norm_g[0], scale, shift)

    per_group = group // tn
    pooled = _matmul(hn5, a_w_in[0], k=d, n=pool_width,
                     x_block=(None, tm, d), x_map=lambda j, i: (1 + j // per_group, i, 0),
                     w_block=(d, tn), w_map=w2(0),
                     out_dtype=_BF16, epilogue=_ep_cast, tm=tm, tn=tn, name="in_proj_pool")
    sz = _matmul(hn5, a_w_in[0], k=d, n=pool_width,
                 x_block=(None, tm, d), x_map=lambda j, i: (0, i, 0),
                 w_block=(d, tn), w_map=w2(pool_width // tn),
                 out_dtype=_BF16, epilogue=_ep_silu, tm=tm, tn=tn, name="in_proj_gate")
    y = _matmul(pooled, a_w_group[0], k=group, n=pool_width,
                x_block=(tm, group), x_map=lambda j, i: (i, j // per_group),
                w_block=(None, group, tn), w_map=lambda j, i: (j // per_group, 0, j % per_group),
                out_dtype=_BF16, epilogue=_ep_scale_gate,
                extra=(a_scale[0].reshape(1, pool_width), sz), extra_specs=(vec, tile),
                tm=tm, tn=tn, name="group_linear")
    for kh in range(pool_width // d):
        h = _matmul(y, a_w_out[0], k=d, n=d,
                    x_block=(tm, d), x_map=lambda j, i, kh=kh: (i, kh),
                    w_block=(d, tn), w_map=lambda j, i, kh=kh: (kh, j),
                    out_dtype=_F32, epilogue=_ep_residual,
                    extra=(h, gate), extra_specs=(tile, vec),
                    tm=tm, tn=tn, name=f"out_proj_a{kh}")

    kv_mod = _ada_mod(c, kv_ada_w, kv_ada_b)
    kv_shift, kv_scale = kv_mod[:, :d], kv_mod[:, d:]
    b_mod = _ada_mod(c, b_ada_w[0], b_ada_b[0])
    b_shift, b_scale, b_gate = b_mod[:, :d], b_mod[:, d:2 * d], b_mod[:, 2 * d:]
    kvn, hnb = _norm2(h, kv_norm_g, kv_scale, kv_shift, b_norm_g[0], b_scale, b_shift)

    cos2, sin2 = _rope_tables(positions)
    rope_specs = (pl.BlockSpec((tm, _HEAD_DIM), lambda j, i: (i, 0)),) * 2
    flat = dict(k=d, n=att_width, x_block=(tm, d), x_map=lambda j, i: (i, 0),
                w_block=(d, tn), tm=tm, tn=tn)
    k_rot = _matmul(kvn, w_kv, w_map=w2(0), out_dtype=_BF16,
                    epilogue=functools.partial(_ep_rotary, mult=1.0),
                    extra=(cos2, sin2), extra_specs=rope_specs, name="k_proj", **flat)
    v_val = _matmul(kvn, w_kv, w_map=w2(att_width // tn), out_dtype=_BF16,
                    epilogue=_ep_cast, name="v_proj", **flat)
    q_rot = _matmul(hnb, b_w_qg[0], w_map=w2(0), out_dtype=_BF16,
                    epilogue=functools.partial(_ep_rotary, mult=_HEAD_DIM ** -0.5),
                    extra=(cos2, sin2), extra_specs=rope_specs, name="q_proj", **flat)
    szb = _matmul(hnb, b_w_qg[0], w_map=w2(att_width // tn), out_dtype=_BF16,
                  epilogue=_ep_silu, name="g_proj", **flat)

    layer = n_a
    lam_init = 0.8 - 0.6 * math.exp(-0.3 * layer)
    yb = _attention(q_rot, k_rot, v_val, szb, b_lam[0], b_subln_g[0], lam_init=lam_init)
    h = _matmul(yb, b_w_out[0], w_map=w2(0), out_dtype=_F32, epilogue=_ep_residual,
                extra=(h, b_gate), extra_specs=(tile, vec), name="out_proj_b", **flat)

    return _final_norm(h, final_g)[None]
```

```python
import functools
import math

import jax
import jax.numpy as jnp
from jax import lax
from jax.experimental import pallas as pl
from jax.experimental.pallas import tpu as pltpu

_EPS = 1e-6
_POOL_WINDOWS = (2, 4, 8, 16)
_MAX_WINDOW = max(_POOL_WINDOWS)
_HEAD_DIM = 128
_CHUNK = 64
_ROPE_THETA = 10000.0

_V7X_LANES = 128
_V7X_VMEM_BYTES = 64 * 1024 * 1024
_V7X_VMEM_BUDGET = _V7X_VMEM_BYTES - 8 * 1024 * 1024

_BF16 = jnp.bfloat16
_F32 = jnp.float32


def _params(semantics, vmem_bytes):
    return pltpu.CompilerParams(
        dimension_semantics=semantics,
        vmem_limit_bytes=int(min(max(vmem_bytes, 16 * 1024 * 1024), _V7X_VMEM_BUDGET)),
    )


def _silu(x):
    return x * (1.0 / (1.0 + jnp.exp(-x)))


def _ada_kernel(c_ref, w_ref, b_ref, o_ref):
    sc = _silu(c_ref[...])
    lhs = jnp.broadcast_to(sc, (8, sc.shape[1]))
    acc = jnp.dot(lhs, w_ref[...], preferred_element_type=_F32)
    o_ref[...] = acc[0:1, :] + b_ref[...]


def _ada_mod(c, w, b):
    d, n = w.shape
    tn = 512
    return pl.pallas_call(
        _ada_kernel,
        grid=(n // tn,),
        in_specs=[
            pl.BlockSpec((1, d), lambda j: (0, 0)),
            pl.BlockSpec((d, tn), lambda j: (0, j)),
            pl.BlockSpec((1, tn), lambda j: (0, j)),
        ],
        out_specs=pl.BlockSpec((1, tn), lambda j: (0, j)),
        out_shape=jax.ShapeDtypeStruct((1, n), _F32),
        compiler_params=_params(("arbitrary",), 3 * d * tn * 4),
        name="ada_mod",
    )(c, w, b.reshape(1, n))


def _norm_pool_kernel(x_ref, g_ref, sc_ref, sh_ref, o_ref, carry_ref, *, ts):
    i = pl.program_id(0)

    @pl.when(i == 0)
    def _():
        carry_ref[...] = jnp.zeros_like(carry_ref)

    x = x_ref[...]
    ms = jnp.mean(x * x, axis=-1, keepdims=True)
    hn = (x * lax.rsqrt(ms + _EPS)) * g_ref[...] * (1.0 + sc_ref[...]) + sh_ref[...]
    o_ref[0] = hn.astype(_BF16)

    m = _MAX_WINDOW
    ext = jnp.concatenate([carry_ref[...], hn], axis=0)
    carry_ref[...] = hn[ts - m:, :]

    t = i * ts + lax.broadcasted_iota(jnp.int32, (ts, 1), 0)
    win = ext
    width = 1
    for gidx, w in enumerate(_POOL_WINDOWS):
        while width < w:
            win = win[width:, :] + win[:-width, :]
            width *= 2
        start = m - (width - 1)
        wsum = win[start:start + ts, :]
        inv_cnt = 1.0 / jnp.minimum(t + 1, w).astype(_F32)
        o_ref[1 + gidx] = (wsum * inv_cnt - hn).astype(_BF16)


def _norm_pool(x, g, scale, shift):
    s, d = x.shape
    ts = 128
    n_out = 1 + len(_POOL_WINDOWS)
    vec = pl.BlockSpec((1, d), lambda i: (0, 0))
    return pl.pallas_call(
        functools.partial(_norm_pool_kernel, ts=ts),
        grid=(s // ts,),
        in_specs=[pl.BlockSpec((ts, d), lambda i: (i, 0)), vec, vec, vec],
        out_specs=pl.BlockSpec((n_out, ts, d), lambda i: (0, i, 0)),
        out_shape=jax.ShapeDtypeStruct((n_out, s, d), _BF16),
        scratch_shapes=[pltpu.VMEM((_MAX_WINDOW, d), _F32)],
        compiler_params=_params(("arbitrary",), 2 * ts * d * (4 + 2 * n_out) + 16 * ts * d * 4),
        name="norm_pool",
    )(x, g.reshape(1, d), scale, shift)


def _norm2_kernel(x_ref, g1_ref, sc1_ref, sh1_ref, g2_ref, sc2_ref, sh2_ref, o1_ref, o2_ref):
    x = x_ref[...]
    ms = jnp.mean(x * x, axis=-1, keepdims=True)
    xn = x * lax.rsqrt(ms + _EPS)
    o1_ref[...] = (xn * g1_ref[...] * (1.0 + sc1_ref[...]) + sh1_ref[...]).astype(_BF16)
    o2_ref[...] = (xn * g2_ref[...] * (1.0 + sc2_ref[...]) + sh2_ref[...]).astype(_BF16)


def _norm2(x, g1, sc1, sh1, g2, sc2, sh2):
    s, d = x.shape
    ts = 256
    vec = pl.BlockSpec((1, d), lambda i: (0, 0))
    row = pl.BlockSpec((ts, d), lambda i: (i, 0))
    return pl.pallas_call(
        _norm2_kernel,
        grid=(s // ts,),
        in_specs=[row, vec, vec, vec, vec, vec, vec],
        out_specs=[row, row],
        out_shape=[jax.ShapeDtypeStruct((s, d), _BF16)] * 2,
        compiler_params=_params(("arbitrary",), 2 * ts * d * 8 + 6 * ts * d * 4),
        name="norm2",
    )(x, g1.reshape(1, d), sc1, sh1, g2.reshape(1, d), sc2, sh2)


def _final_norm_kernel(x_ref, g_ref, o_ref):
    x = x_ref[...]
    ms = jnp.mean(x * x, axis=-1, keepdims=True)
    o_ref[...] = x * lax.rsqrt(ms + _EPS) * g_ref[...]


def _final_norm(x, g):
    s, d = x.shape
    ts = 256
    row = pl.BlockSpec((ts, d), lambda i: (i, 0))
    return pl.pallas_call(
        _final_norm_kernel,
        grid=(s // ts,),
        in_specs=[row, pl.BlockSpec((1, d), lambda i: (0, 0))],
        out_specs=row,
        out_shape=jax.ShapeDtypeStruct((s, d), _F32),
        compiler_params=_params(("arbitrary",), 4 * ts * d * 4 + 4 * ts * d * 4),
        name="final_norm",
    )(x, g.reshape(1, d))


def _rope_kernel(pos_ref, cos_ref, sin_ref):
    half = _HEAD_DIM // 2
    lane = lax.broadcasted_iota(jnp.int32, (1, _HEAD_DIM), 1)
    f = jnp.where(lane < half, lane, lane - half).astype(_F32)
    inv = jnp.exp(f * (-2.0 * math.log(_ROPE_THETA) / _HEAD_DIM))
    ang = pos_ref[...].astype(_F32) * inv
    cos_ref[...] = jnp.cos(ang)
    sin_ref[...] = jnp.where(lane < half, -1.0, 1.0) * jnp.sin(ang)


def _rope_tables(positions):
    s = positions.shape[-1]
    ts = 1024
    tab = pl.BlockSpec((ts, _HEAD_DIM), lambda i: (i, 0))
    return pl.pallas_call(
        _rope_kernel,
        grid=(s // ts,),
        in_specs=[pl.BlockSpec((ts, 1), lambda i: (i, 0))],
        out_specs=[tab, tab],
        out_shape=[jax.ShapeDtypeStruct((s, _HEAD_DIM), _F32)] * 2,
        compiler_params=_params(("arbitrary",), 16 * ts * _HEAD_DIM * 4),
        name="rope_tables",
    )(positions.reshape(s, 1))


def _mm_kernel(x_ref, w_ref, *rest, epilogue, n_extra):
    extra = rest[:n_extra]
    o_ref = rest[n_extra]
    wbf_ref = rest[n_extra + 1]

    @pl.when(pl.program_id(1) == 0)
    def _():
        wbf_ref[...] = w_ref[...].astype(_BF16)

    acc = jnp.dot(x_ref[...], wbf_ref[...], preferred_element_type=_F32)
    epilogue(acc, extra, o_ref)


def _ep_cast(acc, extra, o_ref):
    o_ref[...] = acc.astype(o_ref.dtype)


def _ep_silu(acc, extra, o_ref):
    o_ref[...] = _silu(acc).astype(o_ref.dtype)


def _ep_scale_gate(acc, extra, o_ref):
    scale_ref, sz_ref = extra
    o_ref[...] = (acc * scale_ref[...] * sz_ref[...].astype(_F32)).astype(o_ref.dtype)


def _ep_residual(acc, extra, o_ref):
    res_ref, gate_ref = extra
    o_ref[...] = res_ref[...] + gate_ref[...] * acc


def _ep_rotary(acc, extra, o_ref, *, mult):
    cos_ref, sin_ref = extra
    cos = cos_ref[...]
    sin = sin_ref[...]
    for c in range(acc.shape[1] // _HEAD_DIM):
        a = acc[:, c * _HEAD_DIM:(c + 1) * _HEAD_DIM]
        rot = a * cos + pltpu.roll(a, _HEAD_DIM // 2, axis=1) * sin
        if mult != 1.0:
            rot = rot * mult
        o_ref[:, c * _HEAD_DIM:(c + 1) * _HEAD_DIM] = rot.astype(o_ref.dtype)


def _ep_transpose(acc, extra, o_ref):
    chunk = o_ref.shape[2]
    for r in range(o_ref.shape[0]):
        o_ref[r] = acc[r * chunk:(r + 1) * chunk, :].T.astype(o_ref.dtype)


def _matmul(x, w, *, k, n, x_map, w_map, w_block, x_block, out_dtype, epilogue,
            extra=(), extra_specs=(), tm, tn, name, out_spec=None, out_shape=None):
    m = x.shape[-2]
    out_bytes = jnp.dtype(out_dtype).itemsize
    vmem = (2 * tm * k * 2 + 2 * k * tn * 4 + k * tn * 2 + 2 * tm * tn * out_bytes
            + 3 * tm * tn * 4 + sum(2 * tm * tn * 4 for _ in extra))
    if out_spec is None:
        out_spec = pl.BlockSpec((tm, tn), lambda j, i: (i, j))
        out_shape = (m, n)
    return pl.pallas_call(
        functools.partial(_mm_kernel, epilogue=epilogue, n_extra=len(extra)),
        grid=(n // tn, m // tm),
        in_specs=[pl.BlockSpec(x_block, x_map), pl.BlockSpec(w_block, w_map), *extra_specs],
        out_specs=out_spec,
        out_shape=jax.ShapeDtypeStruct(out_shape, out_dtype),
        scratch_shapes=[pltpu.VMEM((k, tn), _BF16)],
        compiler_params=_params(("arbitrary", "arbitrary"), vmem),
        name=name,
    )(x, w, *extra)


def _attn_kernel(q_ref, k_ref, vt_ref, sz_ref, lam_ref, g_ref, o_ref,
                 sa_ref, sb_ref, stat_ref, acc_ref, *, tq, tk, lam_init):
    i = pl.program_id(1)
    dh = _HEAD_DIM
    acc_ref[...] = jnp.zeros_like(acc_ref)
    stat_ref[0:2, :] = jnp.full((2, tq), -jnp.inf, _F32)
    stat_ref[2:4, :] = jnp.zeros((2, tq), _F32)

    def scores(chunk, c, s_ref):
        start = pl.multiple_of(chunk * tk, tk)
        s_ref[c] = lax.dot_general(k_ref[pl.ds(start, tk), c * dh:(c + 1) * dh],
                                   q_ref[:, c * dh:(c + 1) * dh],
                                   (((1,), (1,)), ((), ())),
                                   preferred_element_type=_F32)

    def update(chunk, c, s_ref, mask):
        st = s_ref[c]
        if mask is not None:
            st = jnp.where(mask, st, -jnp.inf)
        m_prev = stat_ref[c:c + 1, :]
        m_new = jnp.maximum(m_prev, jnp.max(st, axis=0, keepdims=True))
        alpha = jnp.exp2(m_prev - m_new)
        pt = jnp.exp2(st - m_new)
        stat_ref[2 + c:3 + c, :] = alpha * stat_ref[2 + c:3 + c, :] + jnp.sum(pt, axis=0, keepdims=True)
        stat_ref[c:c + 1, :] = m_new
        pv = jnp.dot(vt_ref[chunk], pt.astype(_BF16), preferred_element_type=_F32)
        acc_ref[c] = alpha * acc_ref[c] + pv

    def stage(cur, s_cur, nxt, s_nxt, mask):
        for c in range(2):
            if s_nxt is not None:
                scores(nxt, c, s_nxt)
            update(cur, c, s_cur, mask)

    for c in range(2):
        scores(0, c, sa_ref)

    def pair(p, carry):
        stage(2 * p, sa_ref, 2 * p + 1, sb_ref, None)
        stage(2 * p + 1, sb_ref, 2 * p + 2, sa_ref, None)
        return carry

    lax.fori_loop(0, i, pair, 0)

    key_chunk = lax.broadcasted_iota(jnp.int32, (tk, tq), 0) // _CHUNK
    qry_chunk = lax.broadcasted_iota(jnp.int32, (tk, tq), 1) // _CHUNK
    stage(2 * i, sa_ref, 2 * i + 1, sb_ref, key_chunk <= qry_chunk)
    stage(2 * i + 1, sb_ref, None, None, key_chunk + tk // _CHUNK <= qry_chunk)

    lp = lam_ref[...]
    lam = (jnp.exp(jnp.sum(lp[0:1] * lp[1:2], axis=1, keepdims=True))
           - jnp.exp(jnp.sum(lp[2:3] * lp[3:4], axis=1, keepdims=True)) + lam_init)
    inv1 = 1.0 / stat_ref[2:3, :]
    inv2 = 1.0 / stat_ref[3:4, :]
    ot = acc_ref[0] * inv1 - lam * (acc_ref[1] * inv2)
    ms = jnp.mean(ot * ot, axis=0, keepdims=True)
    ont = ot * lax.rsqrt(ms + _EPS) * g_ref[...] * (1.0 - lam_init)
    o_ref[...] = (ont.T * sz_ref[...].astype(_F32)).astype(o_ref.dtype)


_ATTN_KEY_CHUNK = 512
_ATTN_QUERY_TILE = 2 * _ATTN_KEY_CHUNK


def _attention(q, k, vt, sz, lam_params, subln_g, *, lam_init):
    s, width = q.shape
    hw = 2 * _HEAD_DIM
    n_heads = width // hw
    tq, tk = _ATTN_QUERY_TILE, _ATTN_KEY_CHUNK
    tile = pl.BlockSpec((tq, hw), lambda h, i: (i, h))
    vmem = (2 * 2 * s * hw * 2 + 3 * 2 * tq * hw * 2 + 2 * hw * tq * 4 + 8 * 2 * tk * tq * 4)
    return pl.pallas_call(
        functools.partial(_attn_kernel, tq=tq, tk=tk, lam_init=lam_init),
        grid=(n_heads, s // tq),
        in_specs=[tile,
                  pl.BlockSpec((s, hw), lambda h, i: (0, h)),
                  pl.BlockSpec((s // tk, hw, tk), lambda h, i: (0, h, 0)),
                  tile,
                  pl.BlockSpec((4, _HEAD_DIM), lambda h, i: (0, 0)),
                  pl.BlockSpec((hw, 1), lambda h, i: (0, 0))],
        out_specs=tile,
        out_shape=jax.ShapeDtypeStruct((s, width), _BF16),
        scratch_shapes=[pltpu.VMEM((2, tk, tq), _F32),
                        pltpu.VMEM((2, tk, tq), _F32),
                        pltpu.VMEM((8, tq), _F32),
                        pltpu.VMEM((2, hw, tq), _F32)],
        compiler_params=_params(("arbitrary", "arbitrary"), vmem),
        name="diff_attention",
    )(q, k, vt, sz, lam_params, subln_g.reshape(hw, 1))


def kernel(x, c, positions, a_norm_g, a_ada_w, a_ada_b, a_w_in, a_w_group, a_scale, a_w_out,
           kv_norm_g, kv_ada_w, kv_ada_b, w_kv,
           b_norm_g, b_ada_w, b_ada_b, b_w_qg, b_lam, b_subln_g, b_w_out, final_g):
    batch, s, d = x.shape
    n_a = a_norm_g.shape[0]
    n_b = b_norm_g.shape[0]
    assert batch == 1 and n_a == 1 and n_b == 1
    pool_width = a_w_group.shape[1] * a_w_group.shape[2]
    group = a_w_group.shape[2]
    att_width = b_w_out.shape[1]
    assert a_w_group.shape[1] == len(_POOL_WINDOWS)

    h = x[0]
    tm, tn = 1024, 512
    col = lambda j, i: (0, j)
    tile = pl.BlockSpec((tm, tn), lambda j, i: (i, j))
    vec = pl.BlockSpec((1, tn), col)
    w2 = lambda off: (lambda j, i: (0, j + off))

    mod = _ada_mod(c, a_ada_w[0], a_ada_b[0])
    shift, scale, gate = mod[:, :d], mod[:, d:2 * d], mod[:, 2 * d:]
    hn5 = _norm_pool(h, a_norm_g[0], scale, shift)

    per_group = group // tn
    pooled = _matmul(hn5, a_w_in[0], k=d, n=pool_width,
                     x_block=(None, tm, d), x_map=lambda j, i: (1 + j // per_group, i, 0),
                     w_block=(d, tn), w_map=w2(0),
                     out_dtype=_BF16, epilogue=_ep_cast, tm=tm, tn=tn, name="in_proj_pool")
    sz = _matmul(hn5, a_w_in[0], k=d, n=pool_width,
                 x_block=(None, tm, d), x_map=lambda j, i: (0, i, 0),
                 w_block=(d, tn), w_map=w2(pool_width // tn),
                 out_dtype=_BF16, epilogue=_ep_silu, tm=tm, tn=tn, name="in_proj_gate")
    y = _matmul(pooled, a_w_group[0], k=group, n=pool_width,
                x_block=(tm, group), x_map=lambda j, i: (i, j // per_group),
                w_block=(None, group, tn), w_map=lambda j, i: (j // per_group, 0, j % per_group),
                out_dtype=_BF16, epilogue=_ep_scale_gate,
                extra=(a_scale[0].reshape(1, pool_width), sz), extra_specs=(vec, tile),
                tm=tm, tn=tn, name="group_linear")
    for kh in range(pool_width // d):
        h = _matmul(y, a_w_out[0], k=d, n=d,
                    x_block=(tm, d), x_map=lambda j, i, kh=kh: (i, kh),
                    w_block=(d, tn), w_map=lambda j, i, kh=kh: (kh, j),
                    out_dtype=_F32, epilogue=_ep_residual,
                    extra=(h, gate), extra_specs=(tile, vec),
                    tm=tm, tn=tn, name=f"out_proj_a{kh}")

    kv_mod = _ada_mod(c, kv_ada_w, kv_ada_b)
    kv_shift, kv_scale = kv_mod[:, :d], kv_mod[:, d:]
    b_mod = _ada_mod(c, b_ada_w[0], b_ada_b[0])
    b_shift, b_scale, b_gate = b_mod[:, :d], b_mod[:, d:2 * d], b_mod[:, 2 * d:]
    kvn, hnb = _norm2(h, kv_norm_g, kv_scale, kv_shift, b_norm_g[0], b_scale, b_shift)

    cos2, sin2 = _rope_tables(positions)
    rope_specs = (pl.BlockSpec((tm, _HEAD_DIM), lambda j, i: (i, 0)),) * 2
    flat = dict(k=d, n=att_width, x_block=(tm, d), x_map=lambda j, i: (i, 0),
                w_block=(d, tn), tm=tm, tn=tn)
    k_rot = _matmul(kvn, w_kv, w_map=w2(0), out_dtype=_BF16,
                    epilogue=functools.partial(_ep_rotary, mult=1.0),
                    extra=(cos2, sin2), extra_specs=rope_specs, name="k_proj", **flat)
    tk = _ATTN_KEY_CHUNK
    v_t = _matmul(kvn, w_kv, w_map=w2(att_width // tn), out_dtype=_BF16,
                  epilogue=_ep_transpose, name="v_proj",
                  out_spec=pl.BlockSpec((tm // tk, tn, tk), lambda j, i: (i, j, 0)),
                  out_shape=(s // tk, att_width, tk), **flat)
    q_rot = _matmul(hnb, b_w_qg[0], w_map=w2(0), out_dtype=_BF16,
                    epilogue=functools.partial(_ep_rotary,
                                               mult=_HEAD_DIM ** -0.5 * math.log2(math.e)),
                    extra=(cos2, sin2), extra_specs=rope_specs, name="q_proj", **flat)
    szb = _matmul(hnb, b_w_qg[0], w_map=w2(att_width // tn), out_dtype=_BF16,
                  epilogue=_ep_silu, name="g_proj", **flat)

    layer = n_a
    lam_init = 0.8 - 0.6 * math.exp(-0.3 * layer)
    yb = _attention(q_rot, k_rot, v_t, szb, b_lam[0], b_subln_g[0], lam_init=lam_init)
    h = _matmul(yb, b_w_out[0], w_map=w2(0), out_dtype=_F32, epilogue=_ep_residual,
                extra=(h, b_gate), extra_specs=(tile, vec), name="out_proj_b", **flat)

    return _final_norm(h, final_g)[None]
```

```python
import functools
import math

import jax
import jax.numpy as jnp
from jax import lax
from jax.experimental import pallas as pl
from jax.experimental.pallas import tpu as pltpu

_EPS = 1e-6
_POOL_WINDOWS = (2, 4, 8, 16)
_MAX_WINDOW = max(_POOL_WINDOWS)
_HEAD_DIM = 128
_CHUNK = 64
_ROPE_THETA = 10000.0

_V7X_LANES = 128
_V7X_VMEM_BYTES = 64 * 1024 * 1024
_V7X_VMEM_BUDGET = _V7X_VMEM_BYTES - 8 * 1024 * 1024

_BF16 = jnp.bfloat16
_F32 = jnp.float32


def _params(semantics, vmem_bytes):
    return pltpu.CompilerParams(
        dimension_semantics=semantics,
        vmem_limit_bytes=int(min(max(vmem_bytes, 16 * 1024 * 1024), _V7X_VMEM_BUDGET)),
    )


def _silu(x):
    return x * (1.0 / (1.0 + jnp.exp(-x)))


def _ada_kernel(c_ref, w_ref, b_ref, o_ref):
    sc = _silu(c_ref[...])
    lhs = jnp.broadcast_to(sc, (8, sc.shape[1]))
    acc = jnp.dot(lhs, w_ref[...], preferred_element_type=_F32)
    o_ref[...] = acc[0:1, :] + b_ref[...]


def _ada_mod(c, w, b):
    d, n = w.shape
    tn = 512
    return pl.pallas_call(
        _ada_kernel,
        grid=(n // tn,),
        in_specs=[
            pl.BlockSpec((1, d), lambda j: (0, 0)),
            pl.BlockSpec((d, tn), lambda j: (0, j)),
            pl.BlockSpec((1, tn), lambda j: (0, j)),
        ],
        out_specs=pl.BlockSpec((1, tn), lambda j: (0, j)),
        out_shape=jax.ShapeDtypeStruct((1, n), _F32),
        compiler_params=_params(("arbitrary",), 3 * d * tn * 4),
        name="ada_mod",
    )(c, w, b.reshape(1, n))


_ROW_GROUP = 16
_COL_STRIP = 1024


def _for_row_groups(x_ref, n_rows, emit):
    d = x_ref.shape[-1]

    def body(r, carry):
        r0 = pl.multiple_of(r * _ROW_GROUP, _ROW_GROUP)
        x = x_ref[pl.ds(r0, _ROW_GROUP), :]
        rs = lax.rsqrt(jnp.mean(x * x, axis=-1, keepdims=True) + _EPS)
        for cc in range(d // _COL_STRIP):
            strip = slice(cc * _COL_STRIP, (cc + 1) * _COL_STRIP)
            emit(r0, strip, x_ref[pl.ds(r0, _ROW_GROUP), strip] * rs)
        return carry

    lax.fori_loop(0, n_rows // _ROW_GROUP, body, 0, unroll=2)


def _norm_pool_kernel(x_ref, g_ref, sc_ref, sh_ref, o_ref, ext_ref, coef_ref, *, ts):
    i = pl.program_id(0)
    m = _MAX_WINDOW
    coef_ref[0:1, :] = g_ref[...] * (1.0 + sc_ref[...])

    @pl.when(i == 0)
    def _():
        ext_ref[0:m, :] = jnp.zeros((m, ext_ref.shape[1]), _F32)

    @pl.when(i != 0)
    def _():
        ext_ref[0:m, :] = ext_ref[ts:ts + m, :]

    def emit(r0, strip, xn):
        hn = xn * coef_ref[0:1, strip] + sh_ref[:, strip]
        ext_ref[pl.ds(pl.multiple_of(m + r0, _ROW_GROUP), _ROW_GROUP), strip] = hn
        o_ref[0, pl.ds(r0, _ROW_GROUP), strip] = hn.astype(_BF16)

    _for_row_groups(x_ref, ts, emit)

    n_win = len(_POOL_WINDOWS)
    row = lax.broadcasted_iota(jnp.int32, (n_win * ts, ts + m), 0)
    col = lax.broadcasted_iota(jnp.int32, (n_win * ts, ts + m), 1)
    last = row % ts + m
    width = jnp.zeros_like(row)
    for gidx, w in enumerate(_POOL_WINDOWS):
        width = jnp.where(row // ts == gidx, w, width)
    band = jnp.where((col <= last) & (col > last - width), 1.0, 0.0).astype(_BF16)
    wsum = jnp.dot(band, ext_ref[...].astype(_BF16), preferred_element_type=_F32)

    t = i * ts + lax.broadcasted_iota(jnp.int32, (ts, 1), 0)
    for gidx, w in enumerate(_POOL_WINDOWS):
        inv_cnt = 1.0 / jnp.minimum(t + 1, w).astype(_F32)
        o_ref[1 + gidx] = (wsum[gidx * ts:(gidx + 1) * ts, :] * inv_cnt
                           - ext_ref[m:m + ts, :]).astype(_BF16)


def _norm_pool(x, g, scale, shift):
    s, d = x.shape
    ts = 128
    n_out = 1 + len(_POOL_WINDOWS)
    vec = pl.BlockSpec((1, d), lambda i: (0, 0))
    return pl.pallas_call(
        functools.partial(_norm_pool_kernel, ts=ts),
        grid=(s // ts,),
        in_specs=[pl.BlockSpec((ts, d), lambda i: (i, 0)), vec, vec, vec],
        out_specs=pl.BlockSpec((n_out, ts, d), lambda i: (0, i, 0)),
        out_shape=jax.ShapeDtypeStruct((n_out, s, d), _BF16),
        scratch_shapes=[pltpu.VMEM((ts + _MAX_WINDOW, d), _F32), pltpu.VMEM((8, d), _F32)],
        compiler_params=_params(("arbitrary",), 2 * ts * d * (4 + 2 * n_out) + 12 * ts * d * 4),
        name="norm_pool",
    )(x, g.reshape(1, d), scale, shift)


def _norm2_kernel(x_ref, g1_ref, sc1_ref, sh1_ref, g2_ref, sc2_ref, sh2_ref, o1_ref, o2_ref):
    x = x_ref[...]
    ms = jnp.mean(x * x, axis=-1, keepdims=True)
    xn = x * lax.rsqrt(ms + _EPS)
    o1_ref[...] = (xn * g1_ref[...] * (1.0 + sc1_ref[...]) + sh1_ref[...]).astype(_BF16)
    o2_ref[...] = (xn * g2_ref[...] * (1.0 + sc2_ref[...]) + sh2_ref[...]).astype(_BF16)


def _norm2(x, g1, sc1, sh1, g2, sc2, sh2):
    s, d = x.shape
    ts = 256
    vec = pl.BlockSpec((1, d), lambda i: (0, 0))
    row = pl.BlockSpec((ts, d), lambda i: (i, 0))
    return pl.pallas_call(
        _norm2_kernel,
        grid=(s // ts,),
        in_specs=[row, vec, vec, vec, vec, vec, vec],
        out_specs=[row, row],
        out_shape=[jax.ShapeDtypeStruct((s, d), _BF16)] * 2,
        compiler_params=_params(("arbitrary",), 2 * ts * d * 8 + 6 * ts * d * 4),
        name="norm2",
    )(x, g1.reshape(1, d), sc1, sh1, g2.reshape(1, d), sc2, sh2)


def _final_norm_kernel(x_ref, g_ref, o_ref):
    x = x_ref[...]
    ms = jnp.mean(x * x, axis=-1, keepdims=True)
    o_ref[...] = x * lax.rsqrt(ms + _EPS) * g_ref[...]


def _final_norm(x, g):
    s, d = x.shape
    ts = 256
    row = pl.BlockSpec((ts, d), lambda i: (i, 0))
    return pl.pallas_call(
        _final_norm_kernel,
        grid=(s // ts,),
        in_specs=[row, pl.BlockSpec((1, d), lambda i: (0, 0))],
        out_specs=row,
        out_shape=jax.ShapeDtypeStruct((s, d), _F32),
        compiler_params=_params(("arbitrary",), 4 * ts * d * 4 + 4 * ts * d * 4),
        name="final_norm",
    )(x, g.reshape(1, d))


def _rope_kernel(pos_ref, cos_ref, sin_ref):
    half = _HEAD_DIM // 2
    lane = lax.broadcasted_iota(jnp.int32, (1, _HEAD_DIM), 1)
    f = jnp.where(lane < half, lane, lane - half).astype(_F32)
    inv = jnp.exp(f * (-2.0 * math.log(_ROPE_THETA) / _HEAD_DIM))
    ang = pos_ref[...].astype(_F32) * inv
    cos_ref[...] = jnp.cos(ang)
    sin_ref[...] = jnp.where(lane < half, -1.0, 1.0) * jnp.sin(ang)


def _rope_tables(positions):
    s = positions.shape[-1]
    ts = 1024
    tab = pl.BlockSpec((ts, _HEAD_DIM), lambda i: (i, 0))
    return pl.pallas_call(
        _rope_kernel,
        grid=(s // ts,),
        in_specs=[pl.BlockSpec((ts, 1), lambda i: (i, 0))],
        out_specs=[tab, tab],
        out_shape=[jax.ShapeDtypeStruct((s, _HEAD_DIM), _F32)] * 2,
        compiler_params=_params(("arbitrary",), 16 * ts * _HEAD_DIM * 4),
        name="rope_tables",
    )(positions.reshape(s, 1))


def _mm_kernel(x_ref, w_ref, *rest, epilogue, n_extra, cast_w):
    extra = rest[:n_extra]
    o_ref = rest[n_extra]
    if cast_w:
        wbf_ref = rest[n_extra + 1]

        @pl.when(pl.program_id(1) == 0)
        def _():
            wbf_ref[...] = w_ref[...].astype(_BF16)

        w = wbf_ref[...]
    else:
        w = w_ref[...]
    acc = jnp.dot(x_ref[...].astype(_BF16), w, preferred_element_type=_F32)
    epilogue(acc, extra, o_ref)


def _ep_cast(acc, extra, o_ref):
    o_ref[...] = acc.astype(o_ref.dtype)


def _ep_silu(acc, extra, o_ref):
    o_ref[...] = _silu(acc).astype(o_ref.dtype)


def _ep_scale_gate(acc, extra, o_ref):
    scale_ref, sz_ref = extra
    o_ref[...] = (acc * scale_ref[...] * sz_ref[...].astype(_F32)).astype(o_ref.dtype)


def _ep_residual(acc, extra, o_ref):
    res_ref, gate_ref = extra
    o_ref[...] = res_ref[...] + gate_ref[...] * acc


def _ep_rotary(acc, extra, o_ref, *, mult):
    cos_ref, sin_ref = extra
    cos = cos_ref[...]
    sin = sin_ref[...]
    for c in range(acc.shape[1] // _HEAD_DIM):
        a = acc[:, c * _HEAD_DIM:(c + 1) * _HEAD_DIM]
        rot = a * cos + pltpu.roll(a, _HEAD_DIM // 2, axis=1) * sin
        if mult != 1.0:
            rot = rot * mult
        o_ref[:, c * _HEAD_DIM:(c + 1) * _HEAD_DIM] = rot.astype(o_ref.dtype)


def _ep_transpose(acc, extra, o_ref):
    chunk = o_ref.shape[2]
    for r in range(o_ref.shape[0]):
        o_ref[r] = acc[r * chunk:(r + 1) * chunk, :].T.astype(o_ref.dtype)


def _matmul(x, w, *, k, n, x_map, w_map, w_block, x_block, out_dtype, epilogue,
            extra=(), extra_specs=(), tm, tn, name, out_spec=None, out_shape=None):
    m = x.shape[-2]
    cast_w = w.dtype != _BF16
    out_bytes = jnp.dtype(out_dtype).itemsize
    vmem = (2 * tm * k * x.dtype.itemsize + 2 * k * tn * w.dtype.itemsize
            + (k * tn * 2 if cast_w else 0) + 2 * tm * tn * out_bytes
            + 3 * tm * tn * 4 + sum(2 * tm * tn * 4 for _ in extra))
    if out_spec is None:
        out_spec = pl.BlockSpec((tm, tn), lambda j, i: (i, j))
        out_shape = (m, n)
    return pl.pallas_call(
        functools.partial(_mm_kernel, epilogue=epilogue, n_extra=len(extra), cast_w=cast_w),
        grid=(n // tn, m // tm),
        in_specs=[pl.BlockSpec(x_block, x_map), pl.BlockSpec(w_block, w_map), *extra_specs],
        out_specs=out_spec,
        out_shape=jax.ShapeDtypeStruct(out_shape, out_dtype),
        scratch_shapes=[pltpu.VMEM((k, tn), _BF16)] if cast_w else [],
        compiler_params=_params(("arbitrary", "arbitrary"), vmem),
        name=name,
    )(x, w, *extra)


def _attn_kernel(q_ref, k_ref, vt_ref, sz_ref, lam_ref, g_ref, o_ref,
                 sa_ref, sb_ref, stat_ref, acc_ref, *, tq, tk, lam_init):
    i = pl.program_id(1)
    dh = _HEAD_DIM
    acc_ref[...] = jnp.zeros_like(acc_ref)
    stat_ref[0:2, :] = jnp.full((2, tq), -jnp.inf, _F32)
    stat_ref[2:4, :] = jnp.zeros((2, tq), _F32)

    def scores(chunk, c, s_ref):
        start = pl.multiple_of(chunk * tk, tk)
        s_ref[c] = lax.dot_general(k_ref[pl.ds(start, tk), c * dh:(c + 1) * dh],
                                   q_ref[:, c * dh:(c + 1) * dh],
                                   (((1,), (1,)), ((), ())),
                                   preferred_element_type=_F32)

    def update(chunk, c, s_ref, mask):
        st = s_ref[c]
        if mask is not None:
            st = jnp.where(mask, st, -jnp.inf)
        m_prev = stat_ref[c:c + 1, :]
        m_new = jnp.maximum(m_prev, jnp.max(st, axis=0, keepdims=True))
        alpha = jnp.exp2(m_prev - m_new)
        pt = jnp.exp2(st - m_new)
        stat_ref[2 + c:3 + c, :] = alpha * stat_ref[2 + c:3 + c, :] + jnp.sum(pt, axis=0, keepdims=True)
        stat_ref[c:c + 1, :] = m_new
        pv = jnp.dot(vt_ref[chunk], pt.astype(_BF16), preferred_element_type=_F32)
        acc_ref[c] = alpha * acc_ref[c] + pv

    def stage(cur, s_cur, nxt, s_nxt, mask):
        for c in range(2):
            if s_nxt is not None:
                scores(nxt, c, s_nxt)
            update(cur, c, s_cur, mask)

    for c in range(2):
        scores(0, c, sa_ref)

    def pair(p, carry):
        stage(2 * p, sa_ref, 2 * p + 1, sb_ref, None)
        stage(2 * p + 1, sb_ref, 2 * p + 2, sa_ref, None)
        return carry

    lax.fori_loop(0, i, pair, 0)

    key_chunk = lax.broadcasted_iota(jnp.int32, (tk, tq), 0) // _CHUNK
    qry_chunk = lax.broadcasted_iota(jnp.int32, (tk, tq), 1) // _CHUNK
    stage(2 * i, sa_ref, 2 * i + 1, sb_ref, key_chunk <= qry_chunk)
    stage(2 * i + 1, sb_ref, None, None, key_chunk + tk // _CHUNK <= qry_chunk)

    lp = lam_ref[...]
    lam = (jnp.exp(jnp.sum(lp[0:1] * lp[1:2], axis=1, keepdims=True))
           - jnp.exp(jnp.sum(lp[2:3] * lp[3:4], axis=1, keepdims=True)) + lam_init)
    inv1 = 1.0 / stat_ref[2:3, :]
    inv2 = 1.0 / stat_ref[3:4, :]
    ot = acc_ref[0] * inv1 - lam * (acc_ref[1] * inv2)
    ms = jnp.mean(ot * ot, axis=0, keepdims=True)
    ont = ot * lax.rsqrt(ms + _EPS) * g_ref[...] * (1.0 - lam_init)
    o_ref[...] = (ont.T * sz_ref[...].astype(_F32)).astype(o_ref.dtype)


_ATTN_KEY_CHUNK = 512
_ATTN_QUERY_TILE = 2 * _ATTN_KEY_CHUNK


def _attention(q, k, vt, sz, lam_params, subln_g, *, lam_init):
    s, width = q.shape
    hw = 2 * _HEAD_DIM
    n_heads = width // hw
    tq, tk = _ATTN_QUERY_TILE, _ATTN_KEY_CHUNK
    tile = pl.BlockSpec((tq, hw), lambda h, i: (i, h))
    vmem = (2 * 2 * s * hw * 2 + 3 * 2 * tq * hw * 2 + 2 * hw * tq * 4 + 8 * 2 * tk * tq * 4)
    return pl.pallas_call(
        functools.partial(_attn_kernel, tq=tq, tk=tk, lam_init=lam_init),
        grid=(n_heads, s // tq),
        in_specs=[tile,
                  pl.BlockSpec((s, hw), lambda h, i: (0, h)),
                  pl.BlockSpec((s // tk, hw, tk), lambda h, i: (0, h, 0)),
                  tile,
                  pl.BlockSpec((4, _HEAD_DIM), lambda h, i: (0, 0)),
                  pl.BlockSpec((hw, 1), lambda h, i: (0, 0))],
        out_specs=tile,
        out_shape=jax.ShapeDtypeStruct((s, width), _BF16),
        scratch_shapes=[pltpu.VMEM((2, tk, tq), _F32),
                        pltpu.VMEM((2, tk, tq), _F32),
                        pltpu.VMEM((8, tq), _F32),
                        pltpu.VMEM((2, hw, tq), _F32)],
        compiler_params=_params(("arbitrary", "arbitrary"), vmem),
        name="diff_attention",
    )(q, k, vt, sz, lam_params, subln_g.reshape(hw, 1))


def kernel(x, c, positions, a_norm_g, a_ada_w, a_ada_b, a_w_in, a_w_group, a_scale, a_w_out,
           kv_norm_g, kv_ada_w, kv_ada_b, w_kv,
           b_norm_g, b_ada_w, b_ada_b, b_w_qg, b_lam, b_subln_g, b_w_out, final_g):
    batch, s, d = x.shape
    n_a = a_norm_g.shape[0]
    n_b = b_norm_g.shape[0]
    assert batch == 1 and n_a == 1 and n_b == 1
    pool_width = a_w_group.shape[1] * a_w_group.shape[2]
    group = a_w_group.shape[2]
    att_width = b_w_out.shape[1]
    assert a_w_group.shape[1] == len(_POOL_WINDOWS)

    h = x[0]
    tm, tn = 1024, 512
    col = lambda j, i: (0, j)
    tile = pl.BlockSpec((tm, tn), lambda j, i: (i, j))
    vec = pl.BlockSpec((1, tn), col)
    w2 = lambda off: (lambda j, i: (0, j + off))

    mod = _ada_mod(c, a_ada_w[0], a_ada_b[0])
    shift, scale, gate = mod[:, :d], mod[:, d:2 * d], mod[:, 2 * d:]
    hn5 = _norm_pool(h, a_norm_g[0], scale, shift)

    sz = _matmul(hn5, a_w_in[0], k=d, n=pool_width,
                 x_block=(None, tm, d), x_map=lambda j, i: (0, i, 0),
                 w_block=(d, tn), w_map=w2(pool_width // tn),
                 out_dtype=_BF16, epilogue=_ep_silu, tm=tm, tn=tn, name="in_proj_gate")
    per_group = group // tn
    w_fold = _matmul(a_w_in[0], a_w_group[0], k=group, n=pool_width,
                     x_block=(tm, group), x_map=lambda j, i: (i, j // per_group),
                     w_block=(None, group, tn),
                     w_map=lambda j, i: (j // per_group, 0, j % per_group),
                     out_dtype=_BF16, epilogue=_ep_cast, tm=tm, tn=tn, name="fold_group")
    tn2 = 2 * tn
    per_group2 = group // tn2
    y = _matmul(hn5, w_fold, k=d, n=pool_width,
                x_block=(None, tm, d), x_map=lambda j, i: (1 + j // per_group2, i, 0),
                w_block=(d, tn2), w_map=lambda j, i: (0, j),
                out_dtype=_BF16, epilogue=_ep_scale_gate,
                extra=(a_scale[0].reshape(1, pool_width), sz),
                extra_specs=(pl.BlockSpec((1, tn2), col),
                             pl.BlockSpec((tm, tn2), lambda j, i: (i, j))),
                tm=tm, tn=tn2, name="in_proj_pool")
    for kh in range(pool_width // d):
        h = _matmul(y, a_w_out[0], k=d, n=d,
                    x_block=(tm, d), x_map=lambda j, i, kh=kh: (i, kh),
                    w_block=(d, tn), w_map=lambda j, i, kh=kh: (kh, j),
                    out_dtype=_F32, epilogue=_ep_residual,
                    extra=(h, gate), extra_specs=(tile, vec),
                    tm=tm, tn=tn, name=f"out_proj_a{kh}")

    kv_mod = _ada_mod(c, kv_ada_w, kv_ada_b)
    kv_shift, kv_scale = kv_mod[:, :d], kv_mod[:, d:]
    b_mod = _ada_mod(c, b_ada_w[0], b_ada_b[0])
    b_shift, b_scale, b_gate = b_mod[:, :d], b_mod[:, d:2 * d], b_mod[:, 2 * d:]
    kvn, hnb = _norm2(h, kv_norm_g, kv_scale, kv_shift, b_norm_g[0], b_scale, b_shift)

    cos2, sin2 = _rope_tables(positions)
    rope_specs = (pl.BlockSpec((tm, _HEAD_DIM), lambda j, i: (i, 0)),) * 2
    flat = dict(k=d, n=att_width, x_block=(tm, d), x_map=lambda j, i: (i, 0),
                w_block=(d, tn), tm=tm, tn=tn)
    k_rot = _matmul(kvn, w_kv, w_map=w2(0), out_dtype=_BF16,
                    epilogue=functools.partial(_ep_rotary, mult=1.0),
                    extra=(cos2, sin2), extra_specs=rope_specs, name="k_proj", **flat)
    tk = _ATTN_KEY_CHUNK
    v_t = _matmul(kvn, w_kv, w_map=w2(att_width // tn), out_dtype=_BF16,
                  epilogue=_ep_transpose, name="v_proj",
                  out_spec=pl.BlockSpec((tm // tk, tn, tk), lambda j, i: (i, j, 0)),
                  out_shape=(s // tk, att_width, tk), **flat)
    q_rot = _matmul(hnb, b_w_qg[0], w_map=w2(0), out_dtype=_BF16,
                    epilogue=functools.partial(_ep_rotary,
                                               mult=_HEAD_DIM ** -0.5 * math.log2(math.e)),
                    extra=(cos2, sin2), extra_specs=rope_specs, name="q_proj", **flat)
    szb = _matmul(hnb, b_w_qg[0], w_map=w2(att_width // tn), out_dtype=_BF16,
                  epilogue=_ep_silu, name="g_proj", **flat)

    layer = n_a
    lam_init = 0.8 - 0.6 * math.exp(-0.3 * layer)
    yb = _attention(q_rot, k_rot, v_t, szb, b_lam[0], b_subln_g[0], lam_init=lam_init)
    h = _matmul(yb, b_w_out[0], w_map=w2(0), out_dtype=_F32, epilogue=_ep_residual,
                extra=(h, b_gate), extra_specs=(tile, vec), name="out_proj_b", **flat)

    return _final_norm(h, final_g)[None]
```

```python
import functools
import math

import jax
import jax.numpy as jnp
from jax import lax
from jax.experimental import pallas as pl
from jax.experimental.pallas import tpu as pltpu

_EPS = 1e-6
_POOL_WINDOWS = (2, 4, 8, 16)
_MAX_WINDOW = max(_POOL_WINDOWS)
_HEAD_DIM = 128
_CHUNK = 64
_ROPE_THETA = 10000.0

_V7X_LANES = 128
_V7X_VMEM_BYTES = 64 * 1024 * 1024
_V7X_VMEM_BUDGET = _V7X_VMEM_BYTES - 8 * 1024 * 1024

_BF16 = jnp.bfloat16
_F32 = jnp.float32


def _params(semantics, vmem_bytes):
    return pltpu.CompilerParams(
        dimension_semantics=semantics,
        vmem_limit_bytes=int(min(max(vmem_bytes, 16 * 1024 * 1024), _V7X_VMEM_BUDGET)),
    )


def _silu(x):
    return x * (1.0 / (1.0 + jnp.exp(-x)))


def _ada_kernel(c_ref, w_ref, b_ref, o_ref):
    sc = _silu(c_ref[...])
    lhs = jnp.broadcast_to(sc, (8, sc.shape[1]))
    acc = jnp.dot(lhs, w_ref[...], preferred_element_type=_F32)
    o_ref[...] = acc[0:1, :] + b_ref[...]


def _ada_mod(c, w, b):
    d, n = w.shape
    tn = 512
    return pl.pallas_call(
        _ada_kernel,
        grid=(n // tn,),
        in_specs=[
            pl.BlockSpec((1, d), lambda j: (0, 0)),
            pl.BlockSpec((d, tn), lambda j: (0, j)),
            pl.BlockSpec((1, tn), lambda j: (0, j)),
        ],
        out_specs=pl.BlockSpec((1, tn), lambda j: (0, j)),
        out_shape=jax.ShapeDtypeStruct((1, n), _F32),
        compiler_params=_params(("arbitrary",), 3 * d * tn * 4),
        name="ada_mod",
    )(c, w, b.reshape(1, n))


_ROW_GROUP = 16
_COL_STRIP = 1024


def _for_row_groups(x_ref, n_rows, emit):
    d = x_ref.shape[-1]

    def body(r, carry):
        r0 = pl.multiple_of(r * _ROW_GROUP, _ROW_GROUP)
        x = x_ref[pl.ds(r0, _ROW_GROUP), :]
        rs = lax.rsqrt(jnp.mean(x * x, axis=-1, keepdims=True) + _EPS)
        for cc in range(d // _COL_STRIP):
            strip = slice(cc * _COL_STRIP, (cc + 1) * _COL_STRIP)
            emit(r0, strip, x_ref[pl.ds(r0, _ROW_GROUP), strip] * rs)
        return carry

    lax.fori_loop(0, n_rows // _ROW_GROUP, body, 0, unroll=2)


def _norm_pool_kernel(x_ref, g_ref, sc_ref, sh_ref, o_ref, ext_ref, coef_ref, *, ts):
    i = pl.program_id(0)
    m = _MAX_WINDOW
    coef_ref[0:1, :] = g_ref[...] * (1.0 + sc_ref[...])

    @pl.when(i == 0)
    def _():
        ext_ref[0:m, :] = jnp.zeros((m, ext_ref.shape[1]), _F32)

    @pl.when(i != 0)
    def _():
        ext_ref[0:m, :] = ext_ref[ts:ts + m, :]

    def emit(r0, strip, xn):
        hn = xn * coef_ref[0:1, strip] + sh_ref[:, strip]
        ext_ref[pl.ds(pl.multiple_of(m + r0, _ROW_GROUP), _ROW_GROUP), strip] = hn
        o_ref[0, pl.ds(r0, _ROW_GROUP), strip] = hn.astype(_BF16)

    _for_row_groups(x_ref, ts, emit)

    n_win = len(_POOL_WINDOWS)
    row = lax.broadcasted_iota(jnp.int32, (n_win * ts, ts + m), 0)
    col = lax.broadcasted_iota(jnp.int32, (n_win * ts, ts + m), 1)
    last = row % ts + m
    width = jnp.zeros_like(row)
    for gidx, w in enumerate(_POOL_WINDOWS):
        width = jnp.where(row // ts == gidx, w, width)
    band = jnp.where((col <= last) & (col > last - width), 1.0, 0.0).astype(_BF16)
    wsum = jnp.dot(band, ext_ref[...].astype(_BF16), preferred_element_type=_F32)

    t = i * ts + lax.broadcasted_iota(jnp.int32, (ts, 1), 0)
    for gidx, w in enumerate(_POOL_WINDOWS):
        inv_cnt = 1.0 / jnp.minimum(t + 1, w).astype(_F32)
        o_ref[1 + gidx] = (wsum[gidx * ts:(gidx + 1) * ts, :] * inv_cnt
                           - ext_ref[m:m + ts, :]).astype(_BF16)


def _norm_pool(x, g, scale, shift):
    s, d = x.shape
    ts = 128
    n_out = 1 + len(_POOL_WINDOWS)
    vec = pl.BlockSpec((1, d), lambda i: (0, 0))
    return pl.pallas_call(
        functools.partial(_norm_pool_kernel, ts=ts),
        grid=(s // ts,),
        in_specs=[pl.BlockSpec((ts, d), lambda i: (i, 0)), vec, vec, vec],
        out_specs=pl.BlockSpec((n_out, ts, d), lambda i: (0, i, 0)),
        out_shape=jax.ShapeDtypeStruct((n_out, s, d), _BF16),
        scratch_shapes=[pltpu.VMEM((ts + _MAX_WINDOW, d), _F32), pltpu.VMEM((8, d), _F32)],
        compiler_params=_params(("arbitrary",), 2 * ts * d * (4 + 2 * n_out) + 12 * ts * d * 4),
        name="norm_pool",
    )(x, g.reshape(1, d), scale, shift)


def _norm2_kernel(x_ref, g1_ref, sc1_ref, sh1_ref, g2_ref, sc2_ref, sh2_ref, o1_ref, o2_ref):
    x = x_ref[...]
    ms = jnp.mean(x * x, axis=-1, keepdims=True)
    xn = x * lax.rsqrt(ms + _EPS)
    o1_ref[...] = (xn * g1_ref[...] * (1.0 + sc1_ref[...]) + sh1_ref[...]).astype(_BF16)
    o2_ref[...] = (xn * g2_ref[...] * (1.0 + sc2_ref[...]) + sh2_ref[...]).astype(_BF16)


def _norm2(x, g1, sc1, sh1, g2, sc2, sh2):
    s, d = x.shape
    ts = 256
    vec = pl.BlockSpec((1, d), lambda i: (0, 0))
    row = pl.BlockSpec((ts, d), lambda i: (i, 0))
    return pl.pallas_call(
        _norm2_kernel,
        grid=(s // ts,),
        in_specs=[row, vec, vec, vec, vec, vec, vec],
        out_specs=[row, row],
        out_shape=[jax.ShapeDtypeStruct((s, d), _BF16)] * 2,
        compiler_params=_params(("arbitrary",), 2 * ts * d * 8 + 6 * ts * d * 4),
        name="norm2",
    )(x, g1.reshape(1, d), sc1, sh1, g2.reshape(1, d), sc2, sh2)


def _final_norm_kernel(x_ref, g_ref, o_ref):
    x = x_ref[...]
    ms = jnp.mean(x * x, axis=-1, keepdims=True)
    o_ref[...] = x * lax.rsqrt(ms + _EPS) * g_ref[...]


def _final_norm(x, g):
    s, d = x.shape
    ts = 256
    row = pl.BlockSpec((ts, d), lambda i: (i, 0))
    return pl.pallas_call(
        _final_norm_kernel,
        grid=(s // ts,),
        in_specs=[row, pl.BlockSpec((1, d), lambda i: (0, 0))],
        out_specs=row,
        out_shape=jax.ShapeDtypeStruct((s, d), _F32),
        compiler_params=_params(("arbitrary",), 4 * ts * d * 4 + 4 * ts * d * 4),
        name="final_norm",
    )(x, g.reshape(1, d))


def _rope_kernel(pos_ref, cos_ref, sin_ref):
    half = _HEAD_DIM // 2
    lane = lax.broadcasted_iota(jnp.int32, (1, _HEAD_DIM), 1)
    f = jnp.where(lane < half, lane, lane - half).astype(_F32)
    inv = jnp.exp(f * (-2.0 * math.log(_ROPE_THETA) / _HEAD_DIM))
    ang = pos_ref[...].astype(_F32) * inv
    cos_ref[...] = jnp.cos(ang)
    sin_ref[...] = jnp.where(lane < half, -1.0, 1.0) * jnp.sin(ang)


def _rope_tables(positions):
    s = positions.shape[-1]
    ts = 1024
    tab = pl.BlockSpec((ts, _HEAD_DIM), lambda i: (i, 0))
    return pl.pallas_call(
        _rope_kernel,
        grid=(s // ts,),
        in_specs=[pl.BlockSpec((ts, 1), lambda i: (i, 0))],
        out_specs=[tab, tab],
        out_shape=[jax.ShapeDtypeStruct((s, _HEAD_DIM), _F32)] * 2,
        compiler_params=_params(("arbitrary",), 16 * ts * _HEAD_DIM * 4),
        name="rope_tables",
    )(positions.reshape(s, 1))


def _mm_kernel(x_ref, w_ref, *rest, epilogue, n_extra, cast_w, side):
    rest = list(rest)
    extra = [rest.pop(0) for _ in range(n_extra)]
    side_in = rest.pop(0) if side else None
    o_ref = rest.pop(0)
    side_out = rest.pop(0) if side else None
    if cast_w:
        wbf_ref = rest.pop(0)

        @pl.when(pl.program_id(1) == 0)
        def _():
            wbf_ref[...] = w_ref[...].astype(_BF16)

        w = wbf_ref[...]
    else:
        w = w_ref[...]
    acc = jnp.dot(x_ref[...].astype(_BF16), w, preferred_element_type=_F32)
    epilogue(acc, extra, o_ref)
    if side:
        side_out[...] = side_in[...].astype(_BF16)


def _ep_cast(acc, extra, o_ref):
    o_ref[...] = acc.astype(o_ref.dtype)


def _ep_silu(acc, extra, o_ref):
    o_ref[...] = _silu(acc).astype(o_ref.dtype)


def _ep_scale(acc, extra, o_ref):
    (scale_ref,) = extra
    o_ref[...] = (acc * scale_ref[...]).astype(o_ref.dtype)


def _ep_silu_gate(acc, extra, o_ref):
    (mixed_ref,) = extra
    o_ref[...] = (_silu(acc) * mixed_ref[...].astype(_F32)).astype(o_ref.dtype)


def _ep_residual(acc, extra, o_ref):
    res_ref, gate_ref = extra
    o_ref[...] = res_ref[...] + gate_ref[...] * acc


def _ep_rotary(acc, extra, o_ref, *, mult):
    cos_ref, sin_ref = extra
    cos = cos_ref[...]
    sin = sin_ref[...]
    for c in range(acc.shape[1] // _HEAD_DIM):
        a = acc[:, c * _HEAD_DIM:(c + 1) * _HEAD_DIM]
        rot = a * cos + pltpu.roll(a, _HEAD_DIM // 2, axis=1) * sin
        if mult != 1.0:
            rot = rot * mult
        o_ref[:, c * _HEAD_DIM:(c + 1) * _HEAD_DIM] = rot.astype(o_ref.dtype)


def _ep_transpose(acc, extra, o_ref):
    chunk = o_ref.shape[2]
    for r in range(o_ref.shape[0]):
        o_ref[r] = acc[r * chunk:(r + 1) * chunk, :].T.astype(o_ref.dtype)


def _matmul(x, w, *, k, n, x_map, w_map, w_block, x_block, out_dtype, epilogue,
            extra=(), extra_specs=(), tm, tn, name, out_spec=None, out_shape=None,
            side_cast=None):
    m = x.shape[-2]
    cast_w = w.dtype != _BF16
    n_i = m // tm
    steps = (n // tn) * n_i
    out_bytes = jnp.dtype(out_dtype).itemsize
    vmem = (2 * tm * k * x.dtype.itemsize + 2 * k * tn * w.dtype.itemsize
            + (k * tn * 2 if cast_w else 0) + 2 * tm * tn * out_bytes
            + 4 * tm * tn * 4 + sum(2 * tm * tn * 4 for _ in extra))
    if out_spec is None:
        out_spec = pl.BlockSpec((tm, tn), lambda j, i: (i, j))
        out_shape = (m, n)
    in_specs = [pl.BlockSpec(x_block, x_map), pl.BlockSpec(w_block, w_map), *extra_specs]
    out_specs = [out_spec]
    out_shapes = [jax.ShapeDtypeStruct(out_shape, out_dtype)]
    operands = [x, w, *extra]
    if side_cast is not None:
        src, width, col_block = side_cast
        rows = src.shape[0] // steps
        assert rows * steps == src.shape[0] and rows % 16 == 0
        slab = pl.BlockSpec((rows, width), lambda j, i: (j * n_i + i, col_block))
        in_specs.append(slab)
        operands.append(src)
        out_specs.append(pl.BlockSpec((rows, width), lambda j, i: (j * n_i + i, 0)))
        out_shapes.append(jax.ShapeDtypeStruct((src.shape[0], width), _BF16))
        vmem += 2 * rows * width * 6
    res = pl.pallas_call(
        functools.partial(_mm_kernel, epilogue=epilogue, n_extra=len(extra), cast_w=cast_w,
                          side=side_cast is not None),
        grid=(n // tn, n_i),
        in_specs=in_specs,
        out_specs=out_specs,
        out_shape=out_shapes,
        scratch_shapes=[pltpu.VMEM((k, tn), _BF16)] if cast_w else [],
        compiler_params=_params(("arbitrary", "arbitrary"), vmem),
        name=name,
    )(*operands)
    return res if side_cast is not None else res[0]


def _attn_kernel(q_ref, k_ref, vt_ref, sz_ref, lam_ref, g_ref, o_ref,
                 sa_ref, sb_ref, stat_ref, acc_ref, *, tq, tk, lam_init):
    i = pl.program_id(1)
    dh = _HEAD_DIM
    acc_ref[...] = jnp.zeros_like(acc_ref)
    stat_ref[0:2, :] = jnp.full((2, tq), -jnp.inf, _F32)
    stat_ref[2:4, :] = jnp.zeros((2, tq), _F32)

    every = slice(None)

    def scores(chunk, c, s_ref, qs):
        start = pl.multiple_of(chunk * tk, tk)
        s_ref[c, :, qs] = lax.dot_general(k_ref[pl.ds(start, tk), c * dh:(c + 1) * dh],
                                          q_ref[qs, c * dh:(c + 1) * dh],
                                          (((1,), (1,)), ((), ())),
                                          preferred_element_type=_F32)

    def update(chunk, c, s_ref, qs, mask):
        st = s_ref[c, :, qs]
        if mask is not None:
            st = jnp.where(mask, st, -jnp.inf)
        m_prev = stat_ref[c:c + 1, qs]
        m_new = jnp.maximum(m_prev, jnp.max(st, axis=0, keepdims=True))
        alpha = jnp.exp2(m_prev - m_new)
        pt = jnp.exp2(st - m_new)
        stat_ref[2 + c:3 + c, qs] = (alpha * stat_ref[2 + c:3 + c, qs]
                                     + jnp.sum(pt, axis=0, keepdims=True))
        stat_ref[c:c + 1, qs] = m_new
        pv = jnp.dot(vt_ref[chunk], pt.astype(_BF16), preferred_element_type=_F32)
        acc_ref[c, :, qs] = alpha * acc_ref[c, :, qs] + pv

    def stage(cur, s_cur, cur_qs, mask, nxt, s_nxt, nxt_qs):
        for c in range(2):
            if s_nxt is not None:
                scores(nxt, c, s_nxt, nxt_qs)
            update(cur, c, s_cur, cur_qs, mask)

    for c in range(2):
        scores(0, c, sa_ref, every)

    def pair(p, carry):
        stage(2 * p, sa_ref, every, None, 2 * p + 1, sb_ref, every)
        stage(2 * p + 1, sb_ref, every, None, 2 * p + 2, sa_ref, every)
        return carry

    lax.fori_loop(0, i, pair, 0)

    late = slice(tk, tq)
    key_chunk = lax.broadcasted_iota(jnp.int32, (tk, tq), 0) // _CHUNK
    qry_chunk = lax.broadcasted_iota(jnp.int32, (tk, tq), 1) // _CHUNK
    causal = key_chunk <= qry_chunk
    stage(2 * i, sa_ref, every, causal, 2 * i + 1, sb_ref, late)
    stage(2 * i + 1, sb_ref, late, causal[:, :tq - tk], None, None, None)

    lp = lam_ref[...]
    lam = (jnp.exp(jnp.sum(lp[0:1] * lp[1:2], axis=1, keepdims=True))
           - jnp.exp(jnp.sum(lp[2:3] * lp[3:4], axis=1, keepdims=True)) + lam_init)
    inv1 = 1.0 / stat_ref[2:3, :]
    inv2 = 1.0 / stat_ref[3:4, :]
    ot = acc_ref[0] * inv1 - lam * (acc_ref[1] * inv2)
    ms = jnp.mean(ot * ot, axis=0, keepdims=True)
    ont = ot * lax.rsqrt(ms + _EPS) * g_ref[...] * (1.0 - lam_init)
    o_ref[...] = (ont.T * sz_ref[...].astype(_F32)).astype(o_ref.dtype)


_ATTN_KEY_CHUNK = 512
_ATTN_QUERY_TILE = 2 * _ATTN_KEY_CHUNK


def _attention(q, k, vt, sz, lam_params, subln_g, *, lam_init):
    s, width = q.shape
    hw = 2 * _HEAD_DIM
    n_heads = width // hw
    tq, tk = _ATTN_QUERY_TILE, _ATTN_KEY_CHUNK
    tile = pl.BlockSpec((tq, hw), lambda h, i: (i, h))
    vmem = (2 * 2 * s * hw * 2 + 3 * 2 * tq * hw * 2 + 2 * hw * tq * 4 + 8 * 2 * tk * tq * 4)
    return pl.pallas_call(
        functools.partial(_attn_kernel, tq=tq, tk=tk, lam_init=lam_init),
        grid=(n_heads, s // tq),
        in_specs=[tile,
                  pl.BlockSpec((s, hw), lambda h, i: (0, h)),
                  pl.BlockSpec((s // tk, hw, tk), lambda h, i: (0, h, 0)),
                  tile,
                  pl.BlockSpec((4, _HEAD_DIM), lambda h, i: (0, 0)),
                  pl.BlockSpec((hw, 1), lambda h, i: (0, 0))],
        out_specs=tile,
        out_shape=jax.ShapeDtypeStruct((s, width), _BF16),
        scratch_shapes=[pltpu.VMEM((2, tk, tq), _F32),
                        pltpu.VMEM((2, tk, tq), _F32),
                        pltpu.VMEM((8, tq), _F32),
                        pltpu.VMEM((2, hw, tq), _F32)],
        compiler_params=_params(("arbitrary", "arbitrary"), vmem),
        name="diff_attention",
    )(q, k, vt, sz, lam_params, subln_g.reshape(hw, 1))


def kernel(x, c, positions, a_norm_g, a_ada_w, a_ada_b, a_w_in, a_w_group, a_scale, a_w_out,
           kv_norm_g, kv_ada_w, kv_ada_b, w_kv,
           b_norm_g, b_ada_w, b_ada_b, b_w_qg, b_lam, b_subln_g, b_w_out, final_g):
    batch, s, d = x.shape
    n_a = a_norm_g.shape[0]
    n_b = b_norm_g.shape[0]
    assert batch == 1 and n_a == 1 and n_b == 1
    pool_width = a_w_group.shape[1] * a_w_group.shape[2]
    group = a_w_group.shape[2]
    att_width = b_w_out.shape[1]
    assert a_w_group.shape[1] == len(_POOL_WINDOWS)

    h = x[0]
    tm, tn = 1024, 512
    col = lambda j, i: (0, j)
    tile = pl.BlockSpec((tm, tn), lambda j, i: (i, j))
    vec = pl.BlockSpec((1, tn), col)
    w2 = lambda off: (lambda j, i: (0, j + off))

    mod = _ada_mod(c, a_ada_w[0], a_ada_b[0])
    shift, scale, gate = mod[:, :d], mod[:, d:2 * d], mod[:, 2 * d:]
    hn5 = _norm_pool(h, a_norm_g[0], scale, shift)

    per_group = group // tn
    w_fold = _matmul(a_w_in[0], a_w_group[0], k=group, n=pool_width,
                     x_block=(tm, group), x_map=lambda j, i: (i, j // per_group),
                     w_block=(None, group, tn),
                     w_map=lambda j, i: (j // per_group, 0, j % per_group),
                     out_dtype=_BF16, epilogue=_ep_cast, tm=tm, tn=tn, name="fold_group")
    tn2 = 2 * tn
    per_group2 = group // tn2
    wide_tile = pl.BlockSpec((tm, tn2), lambda j, i: (i, j))
    mixed, w_gate = _matmul(hn5, w_fold, k=d, n=pool_width,
                            x_block=(None, tm, d),
                            x_map=lambda j, i: (1 + j // per_group2, i, 0),
                            w_block=(d, tn2), w_map=col,
                            out_dtype=_BF16, epilogue=_ep_scale,
                            extra=(a_scale[0].reshape(1, pool_width),),
                            extra_specs=(pl.BlockSpec((1, tn2), col),),
                            tm=tm, tn=tn2, name="in_proj_pool",
                            side_cast=(a_w_in[0], pool_width, 1))
    y = _matmul(hn5, w_gate, k=d, n=pool_width,
                x_block=(None, tm, d), x_map=lambda j, i: (0, i, 0),
                w_block=(d, tn2), w_map=col,
                out_dtype=_BF16, epilogue=_ep_silu_gate,
                extra=(mixed,), extra_specs=(wide_tile,),
                tm=tm, tn=tn2, name="in_proj_gate")
    side_src = (w_kv, b_w_qg[0])
    side_out = []
    for kh in range(pool_width // d):
        h, w_bf = _matmul(y, a_w_out[0], k=d, n=d,
                          x_block=(tm, d), x_map=lambda j, i, kh=kh: (i, kh),
                          w_block=(d, tn), w_map=lambda j, i, kh=kh: (kh, j),
                          out_dtype=_F32, epilogue=_ep_residual,
                          extra=(h, gate), extra_specs=(tile, vec),
                          tm=tm, tn=tn, name=f"out_proj_a{kh}",
                          side_cast=(side_src[kh], 2 * att_width, 0))
        side_out.append(w_bf)
    w_kv_bf, w_qg_bf = side_out

    kv_mod = _ada_mod(c, kv_ada_w, kv_ada_b)
    kv_shift, kv_scale = kv_mod[:, :d], kv_mod[:, d:]
    b_mod = _ada_mod(c, b_ada_w[0], b_ada_b[0])
    b_shift, b_scale, b_gate = b_mod[:, :d], b_mod[:, d:2 * d], b_mod[:, 2 * d:]
    kvn, hnb = _norm2(h, kv_norm_g, kv_scale, kv_shift, b_norm_g[0], b_scale, b_shift)

    cos2, sin2 = _rope_tables(positions)
    rope_specs = (pl.BlockSpec((tm, _HEAD_DIM), lambda j, i: (i, 0)),) * 2
    flat = dict(k=d, n=att_width, x_block=(tm, d), x_map=lambda j, i: (i, 0), tm=tm)
    wide = dict(w_block=(d, tn2), tn=tn2, **flat)
    k_rot = _matmul(kvn, w_kv_bf, w_map=w2(0), out_dtype=_BF16,
                    epilogue=functools.partial(_ep_rotary, mult=1.0),
                    extra=(cos2, sin2), extra_specs=rope_specs, name="k_proj", **wide)
    tk = _ATTN_KEY_CHUNK
    v_t = _matmul(kvn, w_kv_bf, w_map=w2(att_width // tn2), out_dtype=_BF16,
                  epilogue=_ep_transpose, name="v_proj",
                  out_spec=pl.BlockSpec((tm // tk, tn2, tk), lambda j, i: (i, j, 0)),
                  out_shape=(s // tk, att_width, tk), **wide)
    q_rot = _matmul(hnb, w_qg_bf, w_map=w2(0), out_dtype=_BF16,
                    epilogue=functools.partial(_ep_rotary,
                                               mult=_HEAD_DIM ** -0.5 * math.log2(math.e)),
                    extra=(cos2, sin2), extra_specs=rope_specs, name="q_proj", **wide)
    szb = _matmul(hnb, w_qg_bf, w_map=w2(att_width // tn2), out_dtype=_BF16,
                  epilogue=_ep_silu, name="g_proj", **wide)

    layer = n_a
    lam_init = 0.8 - 0.6 * math.exp(-0.3 * layer)
    yb = _attention(q_rot, k_rot, v_t, szb, b_lam[0], b_subln_g[0], lam_init=lam_init)
    h = _matmul(yb, b_w_out[0], w_map=w2(0), out_dtype=_F32, epilogue=_ep_residual,
                extra=(h, b_gate), extra_specs=(tile, vec), name="out_proj_b",
                w_block=(d, tn), tn=tn, **flat)

    return _final_norm(h, final_g)[None]
```

```python
import functools
import math

import jax
import jax.numpy as jnp
from jax import lax
from jax.experimental import pallas as pl
from jax.experimental.pallas import tpu as pltpu

_EPS = 1e-6
_POOL_WINDOWS = (2, 4, 8, 16)
_MAX_WINDOW = max(_POOL_WINDOWS)
_HEAD_DIM = 128
_CHUNK = 64
_ROPE_THETA = 10000.0

_V7X_LANES = 128
_V7X_VMEM_BYTES = 64 * 1024 * 1024
_V7X_VMEM_BUDGET = _V7X_VMEM_BYTES - 8 * 1024 * 1024

_BF16 = jnp.bfloat16
_F32 = jnp.float32


def _params(semantics, vmem_bytes):
    return pltpu.CompilerParams(
        dimension_semantics=semantics,
        vmem_limit_bytes=int(min(max(vmem_bytes, 16 * 1024 * 1024), _V7X_VMEM_BUDGET)),
    )


def _silu(x):
    return x * (1.0 / (1.0 + jnp.exp(-x)))


def _ada_kernel(c_ref, w_ref, b_ref, o_ref):
    sc = _silu(c_ref[...])
    lhs = jnp.broadcast_to(sc, (8, sc.shape[1]))
    acc = jnp.dot(lhs, w_ref[...], preferred_element_type=_F32)
    o_ref[...] = acc[0:1, :] + b_ref[...]


def _ada_mod(c, w, b):
    d, n = w.shape
    tn = 512
    return pl.pallas_call(
        _ada_kernel,
        grid=(n // tn,),
        in_specs=[
            pl.BlockSpec((1, d), lambda j: (0, 0)),
            pl.BlockSpec((d, tn), lambda j: (0, j)),
            pl.BlockSpec((1, tn), lambda j: (0, j)),
        ],
        out_specs=pl.BlockSpec((1, tn), lambda j: (0, j)),
        out_shape=jax.ShapeDtypeStruct((1, n), _F32),
        compiler_params=_params(("arbitrary",), 3 * d * tn * 4),
        name="ada_mod",
    )(c, w, b.reshape(1, n))


_ROW_GROUP = 16
_COL_STRIP = 1024


def _for_row_groups(x_ref, n_rows, emit):
    d = x_ref.shape[-1]

    def body(r, carry):
        r0 = pl.multiple_of(r * _ROW_GROUP, _ROW_GROUP)
        x = x_ref[pl.ds(r0, _ROW_GROUP), :]
        rs = lax.rsqrt(jnp.mean(x * x, axis=-1, keepdims=True) + _EPS)
        for cc in range(d // _COL_STRIP):
            strip = slice(cc * _COL_STRIP, (cc + 1) * _COL_STRIP)
            emit(r0, strip, x_ref[pl.ds(r0, _ROW_GROUP), strip] * rs)
        return carry

    lax.fori_loop(0, n_rows // _ROW_GROUP, body, 0, unroll=2)


def _norm_pool_kernel(x_ref, g_ref, sc_ref, sh_ref, o_ref, ext_ref, coef_ref, *, ts):
    i = pl.program_id(0)
    m = _MAX_WINDOW
    coef_ref[0:1, :] = g_ref[...] * (1.0 + sc_ref[...])

    @pl.when(i == 0)
    def _():
        ext_ref[0:m, :] = jnp.zeros((m, ext_ref.shape[1]), _F32)

    @pl.when(i != 0)
    def _():
        ext_ref[0:m, :] = ext_ref[ts:ts + m, :]

    def emit(r0, strip, xn):
        hn = xn * coef_ref[0:1, strip] + sh_ref[:, strip]
        ext_ref[pl.ds(pl.multiple_of(m + r0, _ROW_GROUP), _ROW_GROUP), strip] = hn
        o_ref[0, pl.ds(r0, _ROW_GROUP), strip] = hn.astype(_BF16)

    _for_row_groups(x_ref, ts, emit)

    n_win = len(_POOL_WINDOWS)
    row = lax.broadcasted_iota(jnp.int32, (n_win * ts, ts + m), 0)
    col = lax.broadcasted_iota(jnp.int32, (n_win * ts, ts + m), 1)
    last = row % ts + m
    width = jnp.zeros_like(row)
    for gidx, w in enumerate(_POOL_WINDOWS):
        width = jnp.where(row // ts == gidx, w, width)
    band = jnp.where((col <= last) & (col > last - width), 1.0, 0.0).astype(_BF16)
    wsum = jnp.dot(band, ext_ref[...].astype(_BF16), preferred_element_type=_F32)

    t = i * ts + lax.broadcasted_iota(jnp.int32, (ts, 1), 0)
    for gidx, w in enumerate(_POOL_WINDOWS):
        inv_cnt = 1.0 / jnp.minimum(t + 1, w).astype(_F32)
        o_ref[1 + gidx] = (wsum[gidx * ts:(gidx + 1) * ts, :] * inv_cnt
                           - ext_ref[m:m + ts, :]).astype(_BF16)


def _norm_pool(x, g, scale, shift):
    s, d = x.shape
    ts = 128
    n_out = 1 + len(_POOL_WINDOWS)
    vec = pl.BlockSpec((1, d), lambda i: (0, 0))
    return pl.pallas_call(
        functools.partial(_norm_pool_kernel, ts=ts),
        grid=(s // ts,),
        in_specs=[pl.BlockSpec((ts, d), lambda i: (i, 0)), vec, vec, vec],
        out_specs=pl.BlockSpec((n_out, ts, d), lambda i: (0, i, 0)),
        out_shape=jax.ShapeDtypeStruct((n_out, s, d), _BF16),
        scratch_shapes=[pltpu.VMEM((ts + _MAX_WINDOW, d), _F32), pltpu.VMEM((8, d), _F32)],
        compiler_params=_params(("arbitrary",), 2 * ts * d * (4 + 2 * n_out) + 12 * ts * d * 4),
        name="norm_pool",
    )(x, g.reshape(1, d), scale, shift)


def _norm2_kernel(x_ref, g1_ref, sc1_ref, sh1_ref, g2_ref, sc2_ref, sh2_ref, o1_ref, o2_ref):
    x = x_ref[...]
    ms = jnp.mean(x * x, axis=-1, keepdims=True)
    xn = x * lax.rsqrt(ms + _EPS)
    o1_ref[...] = (xn * g1_ref[...] * (1.0 + sc1_ref[...]) + sh1_ref[...]).astype(_BF16)
    o2_ref[...] = (xn * g2_ref[...] * (1.0 + sc2_ref[...]) + sh2_ref[...]).astype(_BF16)


def _norm2(x, g1, sc1, sh1, g2, sc2, sh2):
    s, d = x.shape
    ts = 256
    vec = pl.BlockSpec((1, d), lambda i: (0, 0))
    row = pl.BlockSpec((ts, d), lambda i: (i, 0))
    return pl.pallas_call(
        _norm2_kernel,
        grid=(s // ts,),
        in_specs=[row, vec, vec, vec, vec, vec, vec],
        out_specs=[row, row],
        out_shape=[jax.ShapeDtypeStruct((s, d), _BF16)] * 2,
        compiler_params=_params(("arbitrary",), 2 * ts * d * 8 + 6 * ts * d * 4),
        name="norm2",
    )(x, g1.reshape(1, d), sc1, sh1, g2.reshape(1, d), sc2, sh2)


def _final_norm_kernel(x_ref, g_ref, o_ref):
    x = x_ref[...]
    ms = jnp.mean(x * x, axis=-1, keepdims=True)
    o_ref[...] = x * lax.rsqrt(ms + _EPS) * g_ref[...]


def _final_norm(x, g):
    s, d = x.shape
    ts = 256
    row = pl.BlockSpec((ts, d), lambda i: (i, 0))
    return pl.pallas_call(
        _final_norm_kernel,
        grid=(s // ts,),
        in_specs=[row, pl.BlockSpec((1, d), lambda i: (0, 0))],
        out_specs=row,
        out_shape=jax.ShapeDtypeStruct((s, d), _F32),
        compiler_params=_params(("arbitrary",), 4 * ts * d * 4 + 4 * ts * d * 4),
        name="final_norm",
    )(x, g.reshape(1, d))


def _rope_kernel(pos_ref, cos_ref, sin_ref):
    half = _HEAD_DIM // 2
    lane = lax.broadcasted_iota(jnp.int32, (1, _HEAD_DIM), 1)
    f = jnp.where(lane < half, lane, lane - half).astype(_F32)
    inv = jnp.exp(f * (-2.0 * math.log(_ROPE_THETA) / _HEAD_DIM))
    ang = pos_ref[...].astype(_F32) * inv
    cos_ref[...] = jnp.cos(ang)
    sin_ref[...] = jnp.where(lane < half, -1.0, 1.0) * jnp.sin(ang)


def _rope_tables(positions):
    s = positions.shape[-1]
    ts = 1024
    tab = pl.BlockSpec((ts, _HEAD_DIM), lambda i: (i, 0))
    return pl.pallas_call(
        _rope_kernel,
        grid=(s // ts,),
        in_specs=[pl.BlockSpec((ts, 1), lambda i: (i, 0))],
        out_specs=[tab, tab],
        out_shape=[jax.ShapeDtypeStruct((s, _HEAD_DIM), _F32)] * 2,
        compiler_params=_params(("arbitrary",), 16 * ts * _HEAD_DIM * 4),
        name="rope_tables",
    )(positions.reshape(s, 1))


_MM_SUB_ROWS = 512


def _mm_kernel(x_ref, w_ref, *rest, epilogue, n_extra, cast_w, side, n_split):
    rest = list(rest)
    extra = [rest.pop(0) for _ in range(n_extra)]
    side_in = rest.pop(0) if side else None
    o_ref = rest.pop(0)
    side_out = rest.pop(0) if side else None
    if cast_w:
        wbf_ref = rest.pop(0)

        @pl.when(pl.program_id(1) == 0)
        def _():
            wbf_ref[...] = w_ref[...].astype(_BF16)

        w = wbf_ref[...]
    else:
        w = w_ref[...]
    sub = x_ref.shape[0] // n_split
    for r in range(n_split):
        rows = slice(r * sub, (r + 1) * sub)
        acc = jnp.dot(x_ref[rows, :].astype(_BF16), w, preferred_element_type=_F32)
        epilogue(acc, extra, o_ref, rows)
    if side:
        side_out[...] = side_in[...].astype(_BF16)


def _ep_cast(acc, extra, o_ref, rows):
    o_ref[rows, :] = acc.astype(o_ref.dtype)


def _ep_silu(acc, extra, o_ref, rows):
    o_ref[rows, :] = _silu(acc).astype(o_ref.dtype)


def _ep_scale(acc, extra, o_ref, rows):
    (scale_ref,) = extra
    o_ref[rows, :] = (acc * scale_ref[...]).astype(o_ref.dtype)


def _ep_silu_gate(acc, extra, o_ref, rows):
    (mixed_ref,) = extra
    o_ref[rows, :] = (_silu(acc) * mixed_ref[rows, :].astype(_F32)).astype(o_ref.dtype)


def _ep_residual(acc, extra, o_ref, rows):
    res_ref, gate_ref = extra
    o_ref[rows, :] = res_ref[rows, :] + gate_ref[...] * acc


def _ep_rotary(acc, extra, o_ref, rows, *, mult):
    cos_ref, sin_ref = extra
    cos = cos_ref[rows, :]
    sin = sin_ref[rows, :]
    for c in range(acc.shape[1] // _HEAD_DIM):
        lanes = slice(c * _HEAD_DIM, (c + 1) * _HEAD_DIM)
        a = acc[:, lanes]
        rot = a * cos + pltpu.roll(a, _HEAD_DIM // 2, axis=1) * sin
        if mult != 1.0:
            rot = rot * mult
        o_ref[rows, lanes] = rot.astype(o_ref.dtype)


def _ep_transpose(acc, extra, o_ref, rows):
    chunk = o_ref.shape[2]
    first = rows.start // chunk
    for r in range(acc.shape[0] // chunk):
        o_ref[first + r] = acc[r * chunk:(r + 1) * chunk, :].T.astype(o_ref.dtype)


def _matmul(x, w, *, k, n, x_map, w_map, w_block, x_block, out_dtype, epilogue,
            extra=(), extra_specs=(), tm, tn, name, out_spec=None, out_shape=None,
            side_cast=None, w_single_buffer=False):
    m = x.shape[-2]
    cast_w = w.dtype != _BF16
    n_i = m // tm
    steps = (n // tn) * n_i
    out_bytes = jnp.dtype(out_dtype).itemsize
    w_bufs = 1 if w_single_buffer else 2
    vmem = (2 * tm * k * x.dtype.itemsize + w_bufs * k * tn * w.dtype.itemsize
            + (k * tn * 2 if cast_w else 0) + 2 * tm * tn * out_bytes
            + 4 * tm * tn * 4 + sum(2 * tm * tn * 4 for _ in extra))
    if out_spec is None:
        out_spec = pl.BlockSpec((tm, tn), lambda j, i: (i, j))
        out_shape = (m, n)
    w_spec = (pl.BlockSpec(w_block, w_map, pipeline_mode=pl.Buffered(1)) if w_single_buffer
              else pl.BlockSpec(w_block, w_map))
    in_specs = [pl.BlockSpec(x_block, x_map), w_spec, *extra_specs]
    out_specs = [out_spec]
    out_shapes = [jax.ShapeDtypeStruct(out_shape, out_dtype)]
    operands = [x, w, *extra]
    if side_cast is not None:
        src, width, col_block = side_cast
        rows = src.shape[0] // steps
        assert rows * steps == src.shape[0] and rows % 16 == 0
        slab = pl.BlockSpec((rows, width), lambda j, i: (j * n_i + i, col_block))
        in_specs.append(slab)
        operands.append(src)
        out_specs.append(pl.BlockSpec((rows, width), lambda j, i: (j * n_i + i, 0)))
        out_shapes.append(jax.ShapeDtypeStruct((src.shape[0], width), _BF16))
        vmem += 2 * rows * width * 6
    res = pl.pallas_call(
        functools.partial(_mm_kernel, epilogue=epilogue, n_extra=len(extra), cast_w=cast_w,
                          side=side_cast is not None, n_split=max(1, tm // _MM_SUB_ROWS)),
        grid=(n // tn, n_i),
        in_specs=in_specs,
        out_specs=out_specs,
        out_shape=out_shapes,
        scratch_shapes=[pltpu.VMEM((k, tn), _BF16)] if cast_w else [],
        compiler_params=_params(("arbitrary", "arbitrary"), vmem),
        name=name,
    )(*operands)
    return res if side_cast is not None else res[0]


def _attn_kernel(q_ref, k_ref, vt_ref, sz_ref, lam_ref, g_ref, o_ref,
                 sa_ref, sb_ref, stat_ref, acc_ref, *, tq, tk, lam_init):
    i = pl.program_id(1)
    dh = _HEAD_DIM
    acc_ref[...] = jnp.zeros_like(acc_ref)
    stat_ref[0:2, :] = jnp.full((2, tq), -jnp.inf, _F32)
    stat_ref[2:4, :] = jnp.zeros((2, tq), _F32)

    every = slice(None)

    def scores(chunk, c, s_ref, qs):
        start = pl.multiple_of(chunk * tk, tk)
        s_ref[c, :, qs] = lax.dot_general(k_ref[pl.ds(start, tk), c * dh:(c + 1) * dh],
                                          q_ref[qs, c * dh:(c + 1) * dh],
                                          (((1,), (1,)), ((), ())),
                                          preferred_element_type=_F32)

    def update(chunk, c, s_ref, qs, mask):
        st = s_ref[c, :, qs]
        if mask is not None:
            st = jnp.where(mask, st, -jnp.inf)
        m_prev = stat_ref[c:c + 1, qs]
        m_new = jnp.maximum(m_prev, jnp.max(st, axis=0, keepdims=True))
        alpha = jnp.exp2(m_prev - m_new)
        pt = jnp.exp2(st - m_new)
        stat_ref[2 + c:3 + c, qs] = (alpha * stat_ref[2 + c:3 + c, qs]
                                     + jnp.sum(pt, axis=0, keepdims=True))
        stat_ref[c:c + 1, qs] = m_new
        pv = jnp.dot(vt_ref[chunk], pt.astype(_BF16), preferred_element_type=_F32)
        acc_ref[c, :, qs] = alpha * acc_ref[c, :, qs] + pv

    def stage(cur, s_cur, cur_qs, mask, nxt, s_nxt, nxt_qs):
        for c in range(2):
            if s_nxt is not None:
                scores(nxt, c, s_nxt, nxt_qs)
            update(cur, c, s_cur, cur_qs, mask)

    for c in range(2):
        scores(0, c, sa_ref, every)

    def pair(p, carry):
        stage(2 * p, sa_ref, every, None, 2 * p + 1, sb_ref, every)
        stage(2 * p + 1, sb_ref, every, None, 2 * p + 2, sa_ref, every)
        return carry

    lax.fori_loop(0, i, pair, 0)

    late = slice(tk, tq)
    key_chunk = lax.broadcasted_iota(jnp.int32, (tk, tq), 0) // _CHUNK
    qry_chunk = lax.broadcasted_iota(jnp.int32, (tk, tq), 1) // _CHUNK
    causal = key_chunk <= qry_chunk
    stage(2 * i, sa_ref, every, causal, 2 * i + 1, sb_ref, late)
    stage(2 * i + 1, sb_ref, late, causal[:, :tq - tk], None, None, None)

    lp = lam_ref[...]
    lam = (jnp.exp(jnp.sum(lp[0:1] * lp[1:2], axis=1, keepdims=True))
           - jnp.exp(jnp.sum(lp[2:3] * lp[3:4], axis=1, keepdims=True)) + lam_init)
    inv1 = 1.0 / stat_ref[2:3, :]
    inv2 = 1.0 / stat_ref[3:4, :]
    ot = acc_ref[0] * inv1 - lam * (acc_ref[1] * inv2)
    ms = jnp.mean(ot * ot, axis=0, keepdims=True)
    ont = ot * lax.rsqrt(ms + _EPS) * g_ref[...] * (1.0 - lam_init)
    o_ref[...] = (ont.T * sz_ref[...].astype(_F32)).astype(o_ref.dtype)


_ATTN_KEY_CHUNK = 512
_ATTN_QUERY_TILE = 2 * _ATTN_KEY_CHUNK


def _attention(q, k, vt, sz, lam_params, subln_g, *, lam_init):
    s, width = q.shape
    hw = 2 * _HEAD_DIM
    n_heads = width // hw
    tq, tk = _ATTN_QUERY_TILE, _ATTN_KEY_CHUNK
    tile = pl.BlockSpec((tq, hw), lambda h, i: (i, h))
    vmem = (2 * 2 * s * hw * 2 + 3 * 2 * tq * hw * 2 + 2 * hw * tq * 4 + 8 * 2 * tk * tq * 4)
    return pl.pallas_call(
        functools.partial(_attn_kernel, tq=tq, tk=tk, lam_init=lam_init),
        grid=(n_heads, s // tq),
        in_specs=[tile,
                  pl.BlockSpec((s, hw), lambda h, i: (0, h)),
                  pl.BlockSpec((s // tk, hw, tk), lambda h, i: (0, h, 0)),
                  tile,
                  pl.BlockSpec((4, _HEAD_DIM), lambda h, i: (0, 0)),
                  pl.BlockSpec((hw, 1), lambda h, i: (0, 0))],
        out_specs=tile,
        out_shape=jax.ShapeDtypeStruct((s, width), _BF16),
        scratch_shapes=[pltpu.VMEM((2, tk, tq), _F32),
                        pltpu.VMEM((2, tk, tq), _F32),
                        pltpu.VMEM((8, tq), _F32),
                        pltpu.VMEM((2, hw, tq), _F32)],
        compiler_params=_params(("arbitrary", "arbitrary"), vmem),
        name="diff_attention",
    )(q, k, vt, sz, lam_params, subln_g.reshape(hw, 1))


def kernel(x, c, positions, a_norm_g, a_ada_w, a_ada_b, a_w_in, a_w_group, a_scale, a_w_out,
           kv_norm_g, kv_ada_w, kv_ada_b, w_kv,
           b_norm_g, b_ada_w, b_ada_b, b_w_qg, b_lam, b_subln_g, b_w_out, final_g):
    batch, s, d = x.shape
    n_a = a_norm_g.shape[0]
    n_b = b_norm_g.shape[0]
    assert batch == 1 and n_a == 1 and n_b == 1
    pool_width = a_w_group.shape[1] * a_w_group.shape[2]
    group = a_w_group.shape[2]
    att_width = b_w_out.shape[1]
    assert a_w_group.shape[1] == len(_POOL_WINDOWS)

    h = x[0]
    tm, tn = 1024, 512
    col = lambda j, i: (0, j)

    mod = _ada_mod(c, a_ada_w[0], a_ada_b[0])
    shift, scale, gate = mod[:, :d], mod[:, d:2 * d], mod[:, 2 * d:]
    hn5 = _norm_pool(h, a_norm_g[0], scale, shift)

    tn2 = 2 * tn
    per_group2 = group // tn2
    w_fold = _matmul(a_w_in[0], a_w_group[0], k=group, n=pool_width,
                     x_block=(tm, group), x_map=lambda j, i: (i, j // per_group2),
                     w_block=(None, group, tn2),
                     w_map=lambda j, i: (j // per_group2, 0, j % per_group2),
                     out_dtype=_BF16, epilogue=_ep_cast, tm=tm, tn=tn2, name="fold_group")
    wide_tile = pl.BlockSpec((tm, tn2), lambda j, i: (i, j))
    mixed, w_gate = _matmul(hn5, w_fold, k=d, n=pool_width,
                            x_block=(None, tm, d),
                            x_map=lambda j, i: (1 + j // per_group2, i, 0),
                            w_block=(d, tn2), w_map=col,
                            out_dtype=_BF16, epilogue=_ep_scale,
                            extra=(a_scale[0].reshape(1, pool_width),),
                            extra_specs=(pl.BlockSpec((1, tn2), col),),
                            tm=tm, tn=tn2, name="in_proj_pool",
                            side_cast=(a_w_in[0], pool_width, 1))
    y, w_out_a = _matmul(hn5, w_gate, k=d, n=pool_width,
                         x_block=(None, tm, d), x_map=lambda j, i: (0, i, 0),
                         w_block=(d, tn2), w_map=col,
                         out_dtype=_BF16, epilogue=_ep_silu_gate,
                         extra=(mixed,), extra_specs=(wide_tile,),
                         tm=tm, tn=tn2, name="in_proj_gate",
                         side_cast=(a_w_out[0], d, 0))
    tm3, tn3 = tm // 2, 4 * tn
    res_tile = pl.BlockSpec((tm3, tn3), lambda j, i: (i, j))
    res_vec = pl.BlockSpec((1, tn3), col)
    kv_halves = []
    for kh in range(pool_width // d):
        h, w_half = _matmul(y, w_out_a, k=d, n=d,
                            x_block=(tm3, d), x_map=lambda j, i, kh=kh: (i, kh),
                            w_block=(d, tn3), w_map=lambda j, i, kh=kh: (kh, j),
                            out_dtype=_F32, epilogue=_ep_residual,
                            extra=(h, gate), extra_specs=(res_tile, res_vec),
                            tm=tm3, tn=tn3, name=f"out_proj_a{kh}", w_single_buffer=True,
                            side_cast=(w_kv, att_width, kh))
        kv_halves.append(w_half)
    w_k_bf, w_v_bf = kv_halves

    kv_mod = _ada_mod(c, kv_ada_w, kv_ada_b)
    kv_shift, kv_scale = kv_mod[:, :d], kv_mod[:, d:]
    b_mod = _ada_mod(c, b_ada_w[0], b_ada_b[0])
    b_shift, b_scale, b_gate = b_mod[:, :d], b_mod[:, d:2 * d], b_mod[:, 2 * d:]
    kvn, hnb = _norm2(h, kv_norm_g, kv_scale, kv_shift, b_norm_g[0], b_scale, b_shift)

    cos2, sin2 = _rope_tables(positions)
    rope_specs = (pl.BlockSpec((tm, _HEAD_DIM), lambda j, i: (i, 0)),) * 2
    flat = dict(k=d, n=att_width, x_block=(tm, d), x_map=lambda j, i: (i, 0), tm=tm)
    wide = dict(w_block=(d, tn2), tn=tn2, **flat)
    k_rot, w_q_bf = _matmul(kvn, w_k_bf, w_map=col, out_dtype=_BF16,
                            epilogue=functools.partial(_ep_rotary, mult=1.0),
                            extra=(cos2, sin2), extra_specs=rope_specs, name="k_proj",
                            side_cast=(b_w_qg[0], att_width, 0), **wide)
    tk = _ATTN_KEY_CHUNK
    v_t, w_g_bf = _matmul(kvn, w_v_bf, w_map=col, out_dtype=_BF16,
                          epilogue=_ep_transpose, name="v_proj",
                          out_spec=pl.BlockSpec((tm // tk, tn2, tk), lambda j, i: (i, j, 0)),
                          out_shape=(s // tk, att_width, tk),
                          side_cast=(b_w_qg[0], att_width, 1), **wide)
    q_rot, w_out_b = _matmul(hnb, w_q_bf, w_map=col, out_dtype=_BF16,
                             epilogue=functools.partial(
                                 _ep_rotary, mult=_HEAD_DIM ** -0.5 * math.log2(math.e)),
                             extra=(cos2, sin2), extra_specs=rope_specs, name="q_proj",
                             side_cast=(b_w_out[0], d, 0), **wide)
    szb = _matmul(hnb, w_g_bf, w_map=col, out_dtype=_BF16,
                  epilogue=_ep_silu, name="g_proj", **wide)

    layer = n_a
    lam_init = 0.8 - 0.6 * math.exp(-0.3 * layer)
    yb = _attention(q_rot, k_rot, v_t, szb, b_lam[0], b_subln_g[0], lam_init=lam_init)
    h = _matmul(yb, w_out_b, k=att_width, n=d,
                x_block=(tm3, att_width), x_map=lambda j, i: (i, 0),
                w_block=(att_width, tn3), w_map=col,
                out_dtype=_F32, epilogue=_ep_residual,
                extra=(h, b_gate), extra_specs=(res_tile, res_vec),
                tm=tm3, tn=tn3, name="out_proj_b", w_single_buffer=True)

    return _final_norm(h, final_g)[None]
```

```python
import functools
import math

import jax
import jax.numpy as jnp
from jax import lax
from jax.experimental import pallas as pl
from jax.experimental.pallas import tpu as pltpu

_EPS = 1e-6
_POOL_WINDOWS = (2, 4, 8, 16)
_MAX_WINDOW = max(_POOL_WINDOWS)
_HEAD_DIM = 128
_CHUNK = 64
_ROPE_THETA = 10000.0

_V7X_LANES = 128
_V7X_VMEM_BYTES = 64 * 1024 * 1024
_V7X_VMEM_BUDGET = _V7X_VMEM_BYTES - 8 * 1024 * 1024

_BF16 = jnp.bfloat16
_F32 = jnp.float32


def _params(semantics, vmem_bytes):
    return pltpu.CompilerParams(
        dimension_semantics=semantics,
        vmem_limit_bytes=int(min(max(vmem_bytes, 16 * 1024 * 1024), _V7X_VMEM_BUDGET)),
    )


def _silu(x):
    return x * (1.0 / (1.0 + jnp.exp(-x)))


def _ada_kernel(c_ref, w_ref, b_ref, o_ref):
    sc = _silu(c_ref[...])
    lhs = jnp.broadcast_to(sc, (8, sc.shape[1]))
    acc = jnp.dot(lhs, w_ref[...], preferred_element_type=_F32)
    o_ref[...] = acc[0:1, :] + b_ref[...]


def _ada_mod(c, w, b):
    d, n = w.shape
    tn = 512
    return pl.pallas_call(
        _ada_kernel,
        grid=(n // tn,),
        in_specs=[
            pl.BlockSpec((1, d), lambda j: (0, 0)),
            pl.BlockSpec((d, tn), lambda j: (0, j)),
            pl.BlockSpec((1, tn), lambda j: (0, j)),
        ],
        out_specs=pl.BlockSpec((1, tn), lambda j: (0, j)),
        out_shape=jax.ShapeDtypeStruct((1, n), _F32),
        compiler_params=_params(("arbitrary",), 3 * d * tn * 4),
        name="ada_mod",
    )(c, w, b.reshape(1, n))


_ROW_GROUP = 16
_COL_STRIP = 1024


def _for_row_groups(x_ref, n_rows, emit):
    d = x_ref.shape[-1]

    def body(r, carry):
        r0 = pl.multiple_of(r * _ROW_GROUP, _ROW_GROUP)
        x = x_ref[pl.ds(r0, _ROW_GROUP), :]
        rs = lax.rsqrt(jnp.mean(x * x, axis=-1, keepdims=True) + _EPS)
        for cc in range(d // _COL_STRIP):
            strip = slice(cc * _COL_STRIP, (cc + 1) * _COL_STRIP)
            emit(r0, strip, x_ref[pl.ds(r0, _ROW_GROUP), strip] * rs)
        return carry

    lax.fori_loop(0, n_rows // _ROW_GROUP, body, 0, unroll=2)


def _norm_pool_kernel(x_ref, g_ref, sc_ref, sh_ref, o_ref, ext_ref, coef_ref, *, ts):
    i = pl.program_id(0)
    m = _MAX_WINDOW
    coef_ref[0:1, :] = g_ref[...] * (1.0 + sc_ref[...])

    @pl.when(i == 0)
    def _():
        ext_ref[0:m, :] = jnp.zeros((m, ext_ref.shape[1]), _F32)

    @pl.when(i != 0)
    def _():
        ext_ref[0:m, :] = ext_ref[ts:ts + m, :]

    def emit(r0, strip, xn):
        hn = xn * coef_ref[0:1, strip] + sh_ref[:, strip]
        ext_ref[pl.ds(pl.multiple_of(m + r0, _ROW_GROUP), _ROW_GROUP), strip] = hn
        o_ref[0, pl.ds(r0, _ROW_GROUP), strip] = hn.astype(_BF16)

    _for_row_groups(x_ref, ts, emit)

    n_win = len(_POOL_WINDOWS)
    row = lax.broadcasted_iota(jnp.int32, (n_win * ts, ts + m), 0)
    col = lax.broadcasted_iota(jnp.int32, (n_win * ts, ts + m), 1)
    last = row % ts + m
    width = jnp.zeros_like(row)
    for gidx, w in enumerate(_POOL_WINDOWS):
        width = jnp.where(row // ts == gidx, w, width)
    band = jnp.where((col <= last) & (col > last - width), 1.0, 0.0).astype(_BF16)
    wsum = jnp.dot(band, ext_ref[...].astype(_BF16), preferred_element_type=_F32)

    t = i * ts + lax.broadcasted_iota(jnp.int32, (ts, 1), 0)
    for gidx, w in enumerate(_POOL_WINDOWS):
        inv_cnt = 1.0 / jnp.minimum(t + 1, w).astype(_F32)
        o_ref[1 + gidx] = (wsum[gidx * ts:(gidx + 1) * ts, :] * inv_cnt
                           - ext_ref[m:m + ts, :]).astype(_BF16)


def _norm_pool(x, g, scale, shift):
    s, d = x.shape
    ts = 128
    n_out = 1 + len(_POOL_WINDOWS)
    vec = pl.BlockSpec((1, d), lambda i: (0, 0))
    return pl.pallas_call(
        functools.partial(_norm_pool_kernel, ts=ts),
        grid=(s // ts,),
        in_specs=[pl.BlockSpec((ts, d), lambda i: (i, 0)), vec, vec, vec],
        out_specs=pl.BlockSpec((n_out, ts, d), lambda i: (0, i, 0)),
        out_shape=jax.ShapeDtypeStruct((n_out, s, d), _BF16),
        scratch_shapes=[pltpu.VMEM((ts + _MAX_WINDOW, d), _F32), pltpu.VMEM((8, d), _F32)],
        compiler_params=_params(("arbitrary",), 2 * ts * d * (4 + 2 * n_out) + 12 * ts * d * 4),
        name="norm_pool",
    )(x, g.reshape(1, d), scale, shift)


def _norm2_kernel(x_ref, g1_ref, sc1_ref, sh1_ref, g2_ref, sc2_ref, sh2_ref, o1_ref, o2_ref):
    x = x_ref[...]
    ms = jnp.mean(x * x, axis=-1, keepdims=True)
    xn = x * lax.rsqrt(ms + _EPS)
    o1_ref[...] = (xn * g1_ref[...] * (1.0 + sc1_ref[...]) + sh1_ref[...]).astype(_BF16)
    o2_ref[...] = (xn * g2_ref[...] * (1.0 + sc2_ref[...]) + sh2_ref[...]).astype(_BF16)


def _norm2(x, g1, sc1, sh1, g2, sc2, sh2):
    s, d = x.shape
    ts = 256
    vec = pl.BlockSpec((1, d), lambda i: (0, 0))
    row = pl.BlockSpec((ts, d), lambda i: (i, 0))
    return pl.pallas_call(
        _norm2_kernel,
        grid=(s // ts,),
        in_specs=[row, vec, vec, vec, vec, vec, vec],
        out_specs=[row, row],
        out_shape=[jax.ShapeDtypeStruct((s, d), _BF16)] * 2,
        compiler_params=_params(("arbitrary",), 2 * ts * d * 8 + 6 * ts * d * 4),
        name="norm2",
    )(x, g1.reshape(1, d), sc1, sh1, g2.reshape(1, d), sc2, sh2)


def _final_norm_kernel(x_ref, g_ref, o_ref):
    x = x_ref[...]
    ms = jnp.mean(x * x, axis=-1, keepdims=True)
    o_ref[...] = x * lax.rsqrt(ms + _EPS) * g_ref[...]


def _final_norm(x, g):
    s, d = x.shape
    ts = 256
    row = pl.BlockSpec((ts, d), lambda i: (i, 0))
    return pl.pallas_call(
        _final_norm_kernel,
        grid=(s // ts,),
        in_specs=[row, pl.BlockSpec((1, d), lambda i: (0, 0))],
        out_specs=row,
        out_shape=jax.ShapeDtypeStruct((s, d), _F32),
        compiler_params=_params(("arbitrary",), 4 * ts * d * 4 + 4 * ts * d * 4),
        name="final_norm",
    )(x, g.reshape(1, d))


def _rope_kernel(pos_ref, cos_ref, sin_ref):
    half = _HEAD_DIM // 2
    lane = lax.broadcasted_iota(jnp.int32, (1, _HEAD_DIM), 1)
    f = jnp.where(lane < half, lane, lane - half).astype(_F32)
    inv = jnp.exp(f * (-2.0 * math.log(_ROPE_THETA) / _HEAD_DIM))
    ang = pos_ref[...].astype(_F32) * inv
    cos_ref[...] = jnp.cos(ang)
    sin_ref[...] = jnp.where(lane < half, -1.0, 1.0) * jnp.sin(ang)


def _rope_tables(positions):
    s = positions.shape[-1]
    ts = 1024
    tab = pl.BlockSpec((ts, _HEAD_DIM), lambda i: (i, 0))
    return pl.pallas_call(
        _rope_kernel,
        grid=(s // ts,),
        in_specs=[pl.BlockSpec((ts, 1), lambda i: (i, 0))],
        out_specs=[tab, tab],
        out_shape=[jax.ShapeDtypeStruct((s, _HEAD_DIM), _F32)] * 2,
        compiler_params=_params(("arbitrary",), 16 * ts * _HEAD_DIM * 4),
        name="rope_tables",
    )(positions.reshape(s, 1))


_MM_SUB_ROWS = 512


def _mm_kernel(x_ref, w_ref, *rest, epilogue, n_extra, cast_w, side, n_split):
    rest = list(rest)
    extra = [rest.pop(0) for _ in range(n_extra)]
    side_in = rest.pop(0) if side else None
    o_ref = rest.pop(0)
    side_out = rest.pop(0) if side else None
    if cast_w:
        wbf_ref = rest.pop(0)

        @pl.when(pl.program_id(1) == 0)
        def _():
            wbf_ref[...] = w_ref[...].astype(_BF16)

        w = wbf_ref[...]
    else:
        w = w_ref[...]
    sub = x_ref.shape[0] // n_split
    for r in range(n_split):
        rows = slice(r * sub, (r + 1) * sub)
        acc = jnp.dot(x_ref[rows, :].astype(_BF16), w, preferred_element_type=_F32)
        epilogue(acc, extra, o_ref, rows)
    if side:
        side_out[...] = side_in[...].astype(_BF16)


def _ep_cast(acc, extra, o_ref, rows):
    o_ref[rows, :] = acc.astype(o_ref.dtype)


def _ep_silu(acc, extra, o_ref, rows):
    o_ref[rows, :] = _silu(acc).astype(o_ref.dtype)


def _ep_scale(acc, extra, o_ref, rows):
    (scale_ref,) = extra
    o_ref[rows, :] = (acc * scale_ref[...]).astype(o_ref.dtype)


def _ep_silu_gate(acc, extra, o_ref, rows):
    (mixed_ref,) = extra
    o_ref[rows, :] = (_silu(acc) * mixed_ref[rows, :].astype(_F32)).astype(o_ref.dtype)


def _ep_residual(acc, extra, o_ref, rows):
    res_ref, gate_ref = extra
    o_ref[rows, :] = res_ref[rows, :] + gate_ref[...] * acc


def _ep_rotary(acc, extra, o_ref, rows, *, mult):
    cos_ref, sin_ref = extra
    cos = cos_ref[rows, :]
    sin = sin_ref[rows, :]
    for c in range(acc.shape[1] // _HEAD_DIM):
        lanes = slice(c * _HEAD_DIM, (c + 1) * _HEAD_DIM)
        a = acc[:, lanes]
        rot = a * cos + pltpu.roll(a, _HEAD_DIM // 2, axis=1) * sin
        if mult != 1.0:
            rot = rot * mult
        o_ref[rows, lanes] = rot.astype(o_ref.dtype)


def _ep_transpose(acc, extra, o_ref, rows):
    chunk = o_ref.shape[2]
    first = rows.start // chunk
    for r in range(acc.shape[0] // chunk):
        o_ref[first + r] = acc[r * chunk:(r + 1) * chunk, :].T.astype(o_ref.dtype)


def _matmul(x, w, *, k, n, x_map, w_map, w_block, x_block, out_dtype, epilogue,
            extra=(), extra_specs=(), tm, tn, name, out_spec=None, out_shape=None,
            side_cast=None, w_single_buffer=False):
    m = x.shape[-2]
    cast_w = w.dtype != _BF16
    n_i = m // tm
    steps = (n // tn) * n_i
    out_bytes = jnp.dtype(out_dtype).itemsize
    w_bufs = 1 if w_single_buffer else 2
    vmem = (2 * tm * k * x.dtype.itemsize + w_bufs * k * tn * w.dtype.itemsize
            + (k * tn * 2 if cast_w else 0) + 2 * tm * tn * out_bytes
            + 4 * tm * tn * 4 + sum(2 * tm * tn * 4 for _ in extra))
    if out_spec is None:
        out_spec = pl.BlockSpec((tm, tn), lambda j, i: (i, j))
        out_shape = (m, n)
    w_spec = (pl.BlockSpec(w_block, w_map, pipeline_mode=pl.Buffered(1)) if w_single_buffer
              else pl.BlockSpec(w_block, w_map))
    in_specs = [pl.BlockSpec(x_block, x_map), w_spec, *extra_specs]
    out_specs = [out_spec]
    out_shapes = [jax.ShapeDtypeStruct(out_shape, out_dtype)]
    operands = [x, w, *extra]
    if side_cast is not None:
        src, width, col_block = side_cast
        rows = src.shape[0] // steps
        assert rows * steps == src.shape[0] and rows % 16 == 0
        slab = pl.BlockSpec((rows, width), lambda j, i: (j * n_i + i, col_block))
        in_specs.append(slab)
        operands.append(src)
        out_specs.append(pl.BlockSpec((rows, width), lambda j, i: (j * n_i + i, 0)))
        out_shapes.append(jax.ShapeDtypeStruct((src.shape[0], width), _BF16))
        vmem += 2 * rows * width * 6
    res = pl.pallas_call(
        functools.partial(_mm_kernel, epilogue=epilogue, n_extra=len(extra), cast_w=cast_w,
                          side=side_cast is not None, n_split=max(1, tm // _MM_SUB_ROWS)),
        grid=(n // tn, n_i),
        in_specs=in_specs,
        out_specs=out_specs,
        out_shape=out_shapes,
        scratch_shapes=[pltpu.VMEM((k, tn), _BF16)] if cast_w else [],
        compiler_params=_params(("arbitrary", "arbitrary"), vmem),
        name=name,
    )(*operands)
    return res if side_cast is not None else res[0]


def _attn_kernel(q_ref, k_ref, vt_ref, sz_ref, lam_ref, g_ref, o_ref,
                 s0_ref, s1_ref, bias_ref, stat_ref, acc_ref, *, tq, tk, lam_init):
    dh = _HEAD_DIM
    s_refs = (s0_ref, s1_ref)

    @pl.when(pl.program_id(0) == 0)
    def _():
        key_chunk = lax.broadcasted_iota(jnp.int32, (tk, tk), 0) // _CHUNK
        qry_chunk = lax.broadcasted_iota(jnp.int32, (tk, tk), 1) // _CHUNK
        bias_ref[...] = jnp.where(key_chunk <= qry_chunk, 0.0, -jnp.inf)

    lp = lam_ref[...]
    lam = (jnp.exp(jnp.sum(lp[0:1] * lp[1:2], axis=1, keepdims=True))
           - jnp.exp(jnp.sum(lp[2:3] * lp[3:4], axis=1, keepdims=True)) + lam_init)
    g_col = g_ref[...] * (1.0 - lam_init)
    every = slice(0, tq)
    late = slice(tk, tq)

    def tile(i):
        q0 = i * tq
        slot = i % 2
        stat_ref[slot, 0:2, :] = jnp.full((2, tq), -jnp.inf, _F32)
        stat_ref[slot, 2:4, :] = jnp.zeros((2, tq), _F32)
        acc_ref[slot] = jnp.zeros(acc_ref.shape[1:], _F32)

        def scores(chunk, c, qs, diagonal):
            start = chunk * tk if isinstance(chunk, int) else pl.multiple_of(chunk * tk, tk)
            st = lax.dot_general(k_ref[pl.ds(start, tk), c * dh:(c + 1) * dh],
                                 q_ref[q0 + qs.start:q0 + qs.stop, c * dh:(c + 1) * dh],
                                 (((1,), (1,)), ((), ())),
                                 preferred_element_type=_F32)
            if diagonal:
                blk = st[:, :tk] + bias_ref[...]
                st = blk if st.shape[1] == tk else jnp.concatenate([blk, st[:, tk:]], axis=1)
            s_refs[c][:, qs] = st
            stat_ref[slot, 4 + c:5 + c, qs] = jnp.max(st, axis=0, keepdims=True)

        def update(chunk, c, qs):
            st = s_refs[c][:, qs]
            m_prev = stat_ref[slot, c:c + 1, qs]
            m_new = jnp.maximum(m_prev, stat_ref[slot, 4 + c:5 + c, qs])
            alpha = jnp.exp2(m_prev - m_new)
            pt = jnp.exp2(st - m_new)
            stat_ref[slot, 2 + c:3 + c, qs] = (alpha * stat_ref[slot, 2 + c:3 + c, qs]
                                               + jnp.sum(pt, axis=0, keepdims=True))
            stat_ref[slot, c:c + 1, qs] = m_new
            pv = jnp.dot(vt_ref[chunk], pt.astype(_BF16), preferred_element_type=_F32)
            acc_ref[slot, c, :, qs] = alpha * acc_ref[slot, c, :, qs] + pv

        diag_a, diag_b = 2 * i, 2 * i + 1
        scores(diag_b, 0, late, True)
        scores(diag_b, 1, late, True)
        update(diag_b, 0, late)
        scores(diag_a, 0, every, True)
        update(diag_b, 1, late)
        scores(diag_a, 1, every, True)
        update(diag_a, 0, every)
        if i > 0:
            scores(0, 0, every, False)
        update(diag_a, 1, every)

        def pair(p, look_ahead):
            j0, j1 = 2 * p, 2 * p + 1
            scores(j0, 1, every, False)
            update(j0, 0, every)
            scores(j1, 0, every, False)
            update(j0, 1, every)
            scores(j1, 1, every, False)
            update(j1, 0, every)
            if look_ahead:
                scores(j1 + 1, 0, every, False)
            update(j1, 1, every)

        if i > 1:
            def body(p, carry):
                pair(p, True)
                return carry
            lax.fori_loop(0, i - 1, body, 0)
        if i > 0:
            pair(i - 1, False)

        inv1 = 1.0 / stat_ref[slot, 2:3, :]
        inv2 = lam / stat_ref[slot, 3:4, :]
        ot = acc_ref[slot, 0] * inv1 - acc_ref[slot, 1] * inv2
        ms = jnp.mean(ot * ot, axis=0, keepdims=True)
        ont = ot * lax.rsqrt(ms + _EPS) * g_col
        rows = slice(q0, q0 + tq)
        o_ref[rows, :] = (ont.T * sz_ref[rows, :].astype(_F32)).astype(o_ref.dtype)

    for i in range(q_ref.shape[0] // tq):
        tile(i)


_ATTN_KEY_CHUNK = 512
_ATTN_QUERY_TILE = 2 * _ATTN_KEY_CHUNK


def _attention(q, k, vt, sz, lam_params, subln_g, *, lam_init):
    s, width = q.shape
    hw = 2 * _HEAD_DIM
    n_heads = width // hw
    tq, tk = _ATTN_QUERY_TILE, _ATTN_KEY_CHUNK
    head = pl.BlockSpec((s, hw), lambda h: (0, h))
    vmem = 2 * 5 * s * hw * 2 + 4 * hw * tq * 4 + 10 * tk * tq * 4
    return pl.pallas_call(
        functools.partial(_attn_kernel, tq=tq, tk=tk, lam_init=lam_init),
        grid=(n_heads,),
        in_specs=[head, head,
                  pl.BlockSpec((s // tk, hw, tk), lambda h: (0, h, 0)),
                  head,
                  pl.BlockSpec((4, _HEAD_DIM), lambda h: (0, 0)),
                  pl.BlockSpec((hw, 1), lambda h: (0, 0))],
        out_specs=head,
        out_shape=jax.ShapeDtypeStruct((s, width), _BF16),
        scratch_shapes=[pltpu.VMEM((tk, tq), _F32),
                        pltpu.VMEM((tk, tq), _F32),
                        pltpu.VMEM((tk, tk), _F32),
                        pltpu.VMEM((2, 8, tq), _F32),
                        pltpu.VMEM((2, 2, hw, tq), _F32)],
        compiler_params=_params(("arbitrary",), vmem),
        name="diff_attention",
    )(q, k, vt, sz, lam_params, subln_g.reshape(hw, 1))


def kernel(x, c, positions, a_norm_g, a_ada_w, a_ada_b, a_w_in, a_w_group, a_scale, a_w_out,
           kv_norm_g, kv_ada_w, kv_ada_b, w_kv,
           b_norm_g, b_ada_w, b_ada_b, b_w_qg, b_lam, b_subln_g, b_w_out, final_g):
    batch, s, d = x.shape
    n_a = a_norm_g.shape[0]
    n_b = b_norm_g.shape[0]
    assert batch == 1 and n_a == 1 and n_b == 1
    pool_width = a_w_group.shape[1] * a_w_group.shape[2]
    group = a_w_group.shape[2]
    att_width = b_w_out.shape[1]
    assert a_w_group.shape[1] == len(_POOL_WINDOWS)

    h = x[0]
    tm, tn = 1024, 512
    col = lambda j, i: (0, j)

    mod = _ada_mod(c, a_ada_w[0], a_ada_b[0])
    shift, scale, gate = mod[:, :d], mod[:, d:2 * d], mod[:, 2 * d:]
    hn5 = _norm_pool(h, a_norm_g[0], scale, shift)

    tn2 = 2 * tn
    per_group2 = group // tn2
    w_fold = _matmul(a_w_in[0], a_w_group[0], k=group, n=pool_width,
                     x_block=(tm, group), x_map=lambda j, i: (i, j // per_group2),
                     w_block=(None, group, tn2),
                     w_map=lambda j, i: (j // per_group2, 0, j % per_group2),
                     out_dtype=_BF16, epilogue=_ep_cast, tm=tm, tn=tn2, name="fold_group")
    wide_tile = pl.BlockSpec((tm, tn2), lambda j, i: (i, j))
    mixed, w_gate = _matmul(hn5, w_fold, k=d, n=pool_width,
                            x_block=(None, tm, d),
                            x_map=lambda j, i: (1 + j // per_group2, i, 0),
                            w_block=(d, tn2), w_map=col,
                            out_dtype=_BF16, epilogue=_ep_scale,
                            extra=(a_scale[0].reshape(1, pool_width),),
                            extra_specs=(pl.BlockSpec((1, tn2), col),),
                            tm=tm, tn=tn2, name="in_proj_pool",
                            side_cast=(a_w_in[0], pool_width, 1))
    y, w_out_a = _matmul(hn5, w_gate, k=d, n=pool_width,
                         x_block=(None, tm, d), x_map=lambda j, i: (0, i, 0),
                         w_block=(d, tn2), w_map=col,
                         out_dtype=_BF16, epilogue=_ep_silu_gate,
                         extra=(mixed,), extra_specs=(wide_tile,),
                         tm=tm, tn=tn2, name="in_proj_gate",
                         side_cast=(a_w_out[0], d, 0))
    tm3, tn3 = tm // 2, 4 * tn
    res_tile = pl.BlockSpec((tm3, tn3), lambda j, i: (i, j))
    res_vec = pl.BlockSpec((1, tn3), col)
    kv_halves = []
    for kh in range(pool_width // d):
        h, w_half = _matmul(y, w_out_a, k=d, n=d,
                            x_block=(tm3, d), x_map=lambda j, i, kh=kh: (i, kh),
                            w_block=(d, tn3), w_map=lambda j, i, kh=kh: (kh, j),
                            out_dtype=_F32, epilogue=_ep_residual,
                            extra=(h, gate), extra_specs=(res_tile, res_vec),
                            tm=tm3, tn=tn3, name=f"out_proj_a{kh}", w_single_buffer=True,
                            side_cast=(w_kv, att_width, kh))
        kv_halves.append(w_half)
    w_k_bf, w_v_bf = kv_halves

    kv_mod = _ada_mod(c, kv_ada_w, kv_ada_b)
    kv_shift, kv_scale = kv_mod[:, :d], kv_mod[:, d:]
    b_mod = _ada_mod(c, b_ada_w[0], b_ada_b[0])
    b_shift, b_scale, b_gate = b_mod[:, :d], b_mod[:, d:2 * d], b_mod[:, 2 * d:]
    kvn, hnb = _norm2(h, kv_norm_g, kv_scale, kv_shift, b_norm_g[0], b_scale, b_shift)

    cos2, sin2 = _rope_tables(positions)
    rope_specs = (pl.BlockSpec((tm, _HEAD_DIM), lambda j, i: (i, 0)),) * 2
    flat = dict(k=d, n=att_width, x_block=(tm, d), x_map=lambda j, i: (i, 0), tm=tm)
    wide = dict(w_block=(d, tn2), tn=tn2, **flat)
    k_rot, w_q_bf = _matmul(kvn, w_k_bf, w_map=col, out_dtype=_BF16,
                            epilogue=functools.partial(_ep_rotary, mult=1.0),
                            extra=(cos2, sin2), extra_specs=rope_specs, name="k_proj",
                            side_cast=(b_w_qg[0], att_width, 0), **wide)
    tk = _ATTN_KEY_CHUNK
    v_t, w_g_bf = _matmul(kvn, w_v_bf, w_map=col, out_dtype=_BF16,
                          epilogue=_ep_transpose, name="v_proj",
                          out_spec=pl.BlockSpec((tm // tk, tn2, tk), lambda j, i: (i, j, 0)),
                          out_shape=(s // tk, att_width, tk),
                          side_cast=(b_w_qg[0], att_width, 1), **wide)
    q_rot, w_out_b = _matmul(hnb, w_q_bf, w_map=col, out_dtype=_BF16,
                             epilogue=functools.partial(
                                 _ep_rotary, mult=_HEAD_DIM ** -0.5 * math.log2(math.e)),
                             extra=(cos2, sin2), extra_specs=rope_specs, name="q_proj",
                             side_cast=(b_w_out[0], d, 0), **wide)
    szb = _matmul(hnb, w_g_bf, w_map=col, out_dtype=_BF16,
                  epilogue=_ep_silu, name="g_proj", **wide)

    layer = n_a
    lam_init = 0.8 - 0.6 * math.exp(-0.3 * layer)
    yb = _attention(q_rot, k_rot, v_t, szb, b_lam[0], b_subln_g[0], lam_init=lam_init)
    h = _matmul(yb, w_out_b, k=att_width, n=d,
                x_block=(tm3, att_width), x_map=lambda j, i: (i, 0),
                w_block=(att_width, tn3), w_map=col,
                out_dtype=_F32, epilogue=_ep_residual,
                extra=(h, b_gate), extra_specs=(res_tile, res_vec),
                tm=tm3, tn=tn3, name="out_proj_b", w_single_buffer=True)

    return _final_norm(h, final_g)[None]
```

```python
import functools
import math

import jax
import jax.numpy as jnp
from jax import lax
from jax.experimental import pallas as pl
from jax.experimental.pallas import tpu as pltpu

_EPS = 1e-6
_POOL_WINDOWS = (2, 4, 8, 16)
_MAX_WINDOW = max(_POOL_WINDOWS)
_HEAD_DIM = 128
_CHUNK = 64
_ROPE_THETA = 10000.0

_V7X_LANES = 128
_V7X_VMEM_BYTES = 64 * 1024 * 1024
_V7X_VMEM_BUDGET = _V7X_VMEM_BYTES - 8 * 1024 * 1024

_BF16 = jnp.bfloat16
_F32 = jnp.float32


def _params(semantics, vmem_bytes):
    return pltpu.CompilerParams(
        dimension_semantics=semantics,
        vmem_limit_bytes=int(min(max(vmem_bytes, 16 * 1024 * 1024), _V7X_VMEM_BUDGET)),
    )


def _silu(x):
    return x * (1.0 / (1.0 + jnp.exp(-x)))


def _ada_kernel(c_ref, w_ref, b_ref, o_ref):
    sc = _silu(c_ref[...])
    lhs = jnp.broadcast_to(sc, (8, sc.shape[1]))
    acc = jnp.dot(lhs, w_ref[...], preferred_element_type=_F32)
    o_ref[...] = acc[0:1, :] + b_ref[...]


def _ada_mod(c, w, b):
    d, n = w.shape
    tn = 512
    return pl.pallas_call(
        _ada_kernel,
        grid=(n // tn,),
        in_specs=[
            pl.BlockSpec((1, d), lambda j: (0, 0)),
            pl.BlockSpec((d, tn), lambda j: (0, j)),
            pl.BlockSpec((1, tn), lambda j: (0, j)),
        ],
        out_specs=pl.BlockSpec((1, tn), lambda j: (0, j)),
        out_shape=jax.ShapeDtypeStruct((1, n), _F32),
        compiler_params=_params(("arbitrary",), 3 * d * tn * 4),
        name="ada_mod",
    )(c, w, b.reshape(1, n))


_ROW_GROUP = 16
_COL_STRIP = 1024


def _for_row_groups(x_ref, n_rows, emit):
    d = x_ref.shape[-1]

    def body(r, carry):
        r0 = pl.multiple_of(r * _ROW_GROUP, _ROW_GROUP)
        x = x_ref[pl.ds(r0, _ROW_GROUP), :]
        rs = lax.rsqrt(jnp.mean(x * x, axis=-1, keepdims=True) + _EPS)
        for cc in range(d // _COL_STRIP):
            strip = slice(cc * _COL_STRIP, (cc + 1) * _COL_STRIP)
            emit(r0, strip, x_ref[pl.ds(r0, _ROW_GROUP), strip] * rs)
        return carry

    lax.fori_loop(0, n_rows // _ROW_GROUP, body, 0, unroll=2)


def _norm_pool_kernel(x_ref, g_ref, sc_ref, sh_ref, o_ref, ext_ref, coef_ref, *, ts):
    i = pl.program_id(0)
    m = _MAX_WINDOW
    coef_ref[0:1, :] = g_ref[...] * (1.0 + sc_ref[...])

    @pl.when(i == 0)
    def _():
        ext_ref[0:m, :] = jnp.zeros((m, ext_ref.shape[1]), _F32)

    @pl.when(i != 0)
    def _():
        ext_ref[0:m, :] = ext_ref[ts:ts + m, :]

    def emit(r0, strip, xn):
        hn = xn * coef_ref[0:1, strip] + sh_ref[:, strip]
        ext_ref[pl.ds(pl.multiple_of(m + r0, _ROW_GROUP), _ROW_GROUP), strip] = hn
        o_ref[0, pl.ds(r0, _ROW_GROUP), strip] = hn.astype(_BF16)

    _for_row_groups(x_ref, ts, emit)

    n_win = len(_POOL_WINDOWS)
    row = lax.broadcasted_iota(jnp.int32, (n_win * ts, ts + m), 0)
    col = lax.broadcasted_iota(jnp.int32, (n_win * ts, ts + m), 1)
    last = row % ts + m
    width = jnp.zeros_like(row)
    for gidx, w in enumerate(_POOL_WINDOWS):
        width = jnp.where(row // ts == gidx, w, width)
    band = jnp.where((col <= last) & (col > last - width), 1.0, 0.0).astype(_BF16)
    wsum = jnp.dot(band, ext_ref[...].astype(_BF16), preferred_element_type=_F32)

    t = i * ts + lax.broadcasted_iota(jnp.int32, (ts, 1), 0)
    for gidx, w in enumerate(_POOL_WINDOWS):
        inv_cnt = 1.0 / jnp.minimum(t + 1, w).astype(_F32)
        o_ref[1 + gidx] = (wsum[gidx * ts:(gidx + 1) * ts, :] * inv_cnt
                           - ext_ref[m:m + ts, :]).astype(_BF16)


def _norm_pool(x, g, scale, shift):
    s, d = x.shape
    ts = 128
    n_out = 1 + len(_POOL_WINDOWS)
    vec = pl.BlockSpec((1, d), lambda i: (0, 0))
    return pl.pallas_call(
        functools.partial(_norm_pool_kernel, ts=ts),
        grid=(s // ts,),
        in_specs=[pl.BlockSpec((ts, d), lambda i: (i, 0)), vec, vec, vec],
        out_specs=pl.BlockSpec((n_out, ts, d), lambda i: (0, i, 0)),
        out_shape=jax.ShapeDtypeStruct((n_out, s, d), _BF16),
        scratch_shapes=[pltpu.VMEM((ts + _MAX_WINDOW, d), _F32), pltpu.VMEM((8, d), _F32)],
        compiler_params=_params(("arbitrary",), 2 * ts * d * (4 + 2 * n_out) + 12 * ts * d * 4),
        name="norm_pool",
    )(x, g.reshape(1, d), scale, shift)


def _norm2_kernel(x_ref, g1_ref, sc1_ref, sh1_ref, g2_ref, sc2_ref, sh2_ref, o1_ref, o2_ref):
    x = x_ref[...]
    ms = jnp.mean(x * x, axis=-1, keepdims=True)
    xn = x * lax.rsqrt(ms + _EPS)
    o1_ref[...] = (xn * g1_ref[...] * (1.0 + sc1_ref[...]) + sh1_ref[...]).astype(_BF16)
    o2_ref[...] = (xn * g2_ref[...] * (1.0 + sc2_ref[...]) + sh2_ref[...]).astype(_BF16)


def _norm2(x, g1, sc1, sh1, g2, sc2, sh2):
    s, d = x.shape
    ts = 256
    vec = pl.BlockSpec((1, d), lambda i: (0, 0))
    row = pl.BlockSpec((ts, d), lambda i: (i, 0))
    return pl.pallas_call(
        _norm2_kernel,
        grid=(s // ts,),
        in_specs=[row, vec, vec, vec, vec, vec, vec],
        out_specs=[row, row],
        out_shape=[jax.ShapeDtypeStruct((s, d), _BF16)] * 2,
        compiler_params=_params(("arbitrary",), 2 * ts * d * 8 + 6 * ts * d * 4),
        name="norm2",
    )(x, g1.reshape(1, d), sc1, sh1, g2.reshape(1, d), sc2, sh2)


def _final_norm_kernel(x_ref, g_ref, o_ref):
    x = x_ref[...]
    ms = jnp.mean(x * x, axis=-1, keepdims=True)
    o_ref[...] = x * lax.rsqrt(ms + _EPS) * g_ref[...]


def _final_norm(x, g):
    s, d = x.shape
    ts = 256
    row = pl.BlockSpec((ts, d), lambda i: (i, 0))
    return pl.pallas_call(
        _final_norm_kernel,
        grid=(s // ts,),
        in_specs=[row, pl.BlockSpec((1, d), lambda i: (0, 0))],
        out_specs=row,
        out_shape=jax.ShapeDtypeStruct((s, d), _F32),
        compiler_params=_params(("arbitrary",), 4 * ts * d * 4 + 4 * ts * d * 4),
        name="final_norm",
    )(x, g.reshape(1, d))


def _rope_kernel(pos_ref, cos_ref, sin_ref):
    half = _HEAD_DIM // 2
    lane = lax.broadcasted_iota(jnp.int32, (1, _HEAD_DIM), 1)
    f = jnp.where(lane < half, lane, lane - half).astype(_F32)
    inv = jnp.exp(f * (-2.0 * math.log(_ROPE_THETA) / _HEAD_DIM))
    ang = pos_ref[...].astype(_F32) * inv
    cos_ref[...] = jnp.cos(ang)
    sin_ref[...] = jnp.where(lane < half, -1.0, 1.0) * jnp.sin(ang)


def _rope_tables(positions):
    s = positions.shape[-1]
    ts = 1024
    tab = pl.BlockSpec((ts, _HEAD_DIM), lambda i: (i, 0))
    return pl.pallas_call(
        _rope_kernel,
        grid=(s // ts,),
        in_specs=[pl.BlockSpec((ts, 1), lambda i: (i, 0))],
        out_specs=[tab, tab],
        out_shape=[jax.ShapeDtypeStruct((s, _HEAD_DIM), _F32)] * 2,
        compiler_params=_params(("arbitrary",), 16 * ts * _HEAD_DIM * 4),
        name="rope_tables",
    )(positions.reshape(s, 1))


_MM_SUB_ROWS = 256

_TILE_ROWS = 1024
_TILE_COLS_F32 = 512
_TILE_COLS_BF16 = 1024
_RES_TILE_ROWS = 512
_RES_TILE_COLS = 2048


def _mm_kernel(x_ref, w_ref, *rest, epilogue, n_extra, cast_w, side, n_split):
    rest = list(rest)
    extra = [rest.pop(0) for _ in range(n_extra)]
    side_in = rest.pop(0) if side else None
    o_ref = rest.pop(0)
    side_out = rest.pop(0) if side else None
    if cast_w:
        wbf_ref = rest.pop(0)

        @pl.when(pl.program_id(1) == 0)
        def _():
            wbf_ref[...] = w_ref[...].astype(_BF16)

        w_src = wbf_ref
    else:
        w_src = w_ref
    sub = x_ref.shape[0] // n_split
    for r in range(n_split):
        rows = slice(r * sub, (r + 1) * sub)
        acc = jnp.dot(x_ref[rows, :].astype(_BF16), w_src[...], preferred_element_type=_F32)
        epilogue(acc, extra, o_ref, rows)
    if side:
        side_out[...] = side_in[...].astype(_BF16)


def _ep_cast(acc, extra, o_ref, rows):
    o_ref[rows, :] = acc.astype(o_ref.dtype)


def _ep_silu(acc, extra, o_ref, rows):
    o_ref[rows, :] = _silu(acc).astype(o_ref.dtype)


def _ep_scale(acc, extra, o_ref, rows):
    (scale_ref,) = extra
    o_ref[rows, :] = (acc * scale_ref[...]).astype(o_ref.dtype)


def _ep_silu_gate(acc, extra, o_ref, rows):
    (mixed_ref,) = extra
    o_ref[rows, :] = (_silu(acc) * mixed_ref[rows, :].astype(_F32)).astype(o_ref.dtype)


def _ep_residual(acc, extra, o_ref, rows):
    res_ref, gate_ref = extra
    o_ref[rows, :] = res_ref[rows, :] + gate_ref[...] * acc


def _ep_rotary(acc, extra, o_ref, rows, *, mult):
    cos_ref, sin_ref = extra
    cos = cos_ref[rows, :]
    sin = sin_ref[rows, :]
    for c in range(acc.shape[1] // _HEAD_DIM):
        lanes = slice(c * _HEAD_DIM, (c + 1) * _HEAD_DIM)
        a = acc[:, lanes]
        rot = a * cos + pltpu.roll(a, _HEAD_DIM // 2, axis=1) * sin
        if mult != 1.0:
            rot = rot * mult
        o_ref[rows, lanes] = rot.astype(o_ref.dtype)


def _ep_transpose(acc, extra, o_ref, rows):
    chunk = o_ref.shape[2]
    n = acc.shape[0]
    first, offset = divmod(rows.start, chunk)
    if n < chunk:
        o_ref[first, :, offset:offset + n] = acc.T.astype(o_ref.dtype)
    else:
        for r in range(n // chunk):
            o_ref[first + r] = acc[r * chunk:(r + 1) * chunk, :].T.astype(o_ref.dtype)


def _matmul(x, w, *, k, n, x_map, w_map, w_block, x_block, out_dtype, epilogue,
            extra=(), extra_specs=(), tm, tn, name, out_spec=None, out_shape=None,
            side_cast=None, w_single_buffer=False):
    m = x.shape[-2]
    cast_w = w.dtype != _BF16
    n_i = m // tm
    steps = (n // tn) * n_i
    out_bytes = jnp.dtype(out_dtype).itemsize
    w_bufs = 1 if w_single_buffer else 2
    vmem = (2 * tm * k * x.dtype.itemsize + w_bufs * k * tn * w.dtype.itemsize
            + (k * tn * 2 if cast_w else 0) + 2 * tm * tn * out_bytes
            + 4 * tm * tn * 4 + sum(2 * tm * tn * 4 for _ in extra))
    if out_spec is None:
        out_spec = pl.BlockSpec((tm, tn), lambda j, i: (i, j))
        out_shape = (m, n)
    w_spec = (pl.BlockSpec(w_block, w_map, pipeline_mode=pl.Buffered(1)) if w_single_buffer
              else pl.BlockSpec(w_block, w_map))
    in_specs = [pl.BlockSpec(x_block, x_map), w_spec, *extra_specs]
    out_specs = [out_spec]
    out_shapes = [jax.ShapeDtypeStruct(out_shape, out_dtype)]
    operands = [x, w, *extra]
    if side_cast is not None:
        src, width, col_block = side_cast
        rows = src.shape[0] // steps
        assert rows * steps == src.shape[0] and rows % 16 == 0
        slab = pl.BlockSpec((rows, width), lambda j, i: (j * n_i + i, col_block))
        in_specs.append(slab)
        operands.append(src)
        out_specs.append(pl.BlockSpec((rows, width), lambda j, i: (j * n_i + i, 0)))
        out_shapes.append(jax.ShapeDtypeStruct((src.shape[0], width), _BF16))
        vmem += 2 * rows * width * 6
    res = pl.pallas_call(
        functools.partial(_mm_kernel, epilogue=epilogue, n_extra=len(extra), cast_w=cast_w,
                          side=side_cast is not None, n_split=max(1, tm // _MM_SUB_ROWS)),
        grid=(n // tn, n_i),
        in_specs=in_specs,
        out_specs=out_specs,
        out_shape=out_shapes,
        scratch_shapes=[pltpu.VMEM((k, tn), _BF16)] if cast_w else [],
        compiler_params=_params(("arbitrary", "arbitrary"), vmem),
        name=name,
    )(*operands)
    return res if side_cast is not None else res[0]


def _attn_kernel(q_ref, k_ref, vt_ref, sz_ref, lam_ref, g_ref, o_ref,
                 s0_ref, s1_ref, bias_ref, stat_ref, acc_ref, *, tq, tk, lam_init):
    dh = _HEAD_DIM
    s_refs = (s0_ref, s1_ref)

    @pl.when(pl.program_id(0) == 0)
    def _():
        key_chunk = lax.broadcasted_iota(jnp.int32, (tk, tk), 0) // _CHUNK
        qry_chunk = lax.broadcasted_iota(jnp.int32, (tk, tk), 1) // _CHUNK
        bias_ref[...] = jnp.where(key_chunk <= qry_chunk, 0.0, -jnp.inf)

    lp = lam_ref[...]
    lam = (jnp.exp(jnp.sum(lp[0:1] * lp[1:2], axis=1, keepdims=True))
           - jnp.exp(jnp.sum(lp[2:3] * lp[3:4], axis=1, keepdims=True)) + lam_init)
    g_col = g_ref[...] * (1.0 - lam_init)
    every = slice(0, tq)
    late = slice(tk, tq)

    def tile(i):
        q0 = i * tq
        slot = i % 2
        stat_ref[slot, 0:2, :] = jnp.full((2, tq), -jnp.inf, _F32)
        stat_ref[slot, 2:4, :] = jnp.zeros((2, tq), _F32)
        acc_ref[slot] = jnp.zeros(acc_ref.shape[1:], _F32)

        def scores(chunk, c, qs, diagonal):
            start = chunk * tk if isinstance(chunk, int) else pl.multiple_of(chunk * tk, tk)
            st = lax.dot_general(k_ref[pl.ds(start, tk), c * dh:(c + 1) * dh],
                                 q_ref[q0 + qs.start:q0 + qs.stop, c * dh:(c + 1) * dh],
                                 (((1,), (1,)), ((), ())),
                                 preferred_element_type=_F32)
            if diagonal:
                blk = st[:, :tk] + bias_ref[...]
                st = blk if st.shape[1] == tk else jnp.concatenate([blk, st[:, tk:]], axis=1)
            s_refs[c][:, qs] = st
            stat_ref[slot, 4 + c:5 + c, qs] = jnp.max(st, axis=0, keepdims=True)

        def update(chunk, c, qs):
            st = s_refs[c][:, qs]
            m_prev = stat_ref[slot, c:c + 1, qs]
            m_new = jnp.maximum(m_prev, stat_ref[slot, 4 + c:5 + c, qs])
            alpha = jnp.exp2(m_prev - m_new)
            pt = jnp.exp2(st - m_new)
            stat_ref[slot, 2 + c:3 + c, qs] = (alpha * stat_ref[slot, 2 + c:3 + c, qs]
                                               + jnp.sum(pt, axis=0, keepdims=True))
            stat_ref[slot, c:c + 1, qs] = m_new
            pv = jnp.dot(vt_ref[chunk], pt.astype(_BF16), preferred_element_type=_F32)
            acc_ref[slot, c, :, qs] = alpha * acc_ref[slot, c, :, qs] + pv

        diag_a, diag_b = 2 * i, 2 * i + 1
        scores(diag_b, 0, late, True)
        scores(diag_b, 1, late, True)
        update(diag_b, 0, late)
        scores(diag_a, 0, every, True)
        update(diag_b, 1, late)
        scores(diag_a, 1, every, True)
        update(diag_a, 0, every)
        if i > 0:
            scores(0, 0, every, False)
        update(diag_a, 1, every)

        def pair(p, look_ahead):
            j0, j1 = 2 * p, 2 * p + 1
            scores(j0, 1, every, False)
            update(j0, 0, every)
            scores(j1, 0, every, False)
            update(j0, 1, every)
            scores(j1, 1, every, False)
            update(j1, 0, every)
            if look_ahead:
                scores(j1 + 1, 0, every, False)
            update(j1, 1, every)

        if i > 1:
            def body(p, carry):
                pair(p, True)
                return carry
            lax.fori_loop(0, i - 1, body, 0)
        if i > 0:
            pair(i - 1, False)

        inv1 = 1.0 / stat_ref[slot, 2:3, :]
        inv2 = lam / stat_ref[slot, 3:4, :]
        ot = acc_ref[slot, 0] * inv1 - acc_ref[slot, 1] * inv2
        ms = jnp.mean(ot * ot, axis=0, keepdims=True)
        ont = ot * lax.rsqrt(ms + _EPS) * g_col
        rows = slice(q0, q0 + tq)
        o_ref[rows, :] = (ont.T * sz_ref[rows, :].astype(_F32)).astype(o_ref.dtype)

    for i in range(q_ref.shape[0] // tq):
        tile(i)


_ATTN_KEY_CHUNK = 512
_ATTN_QUERY_TILE = 2 * _ATTN_KEY_CHUNK


def _attention(q, k, vt, sz, lam_params, subln_g, *, lam_init):
    s, width = q.shape
    hw = 2 * _HEAD_DIM
    n_heads = width // hw
    tq, tk = _ATTN_QUERY_TILE, _ATTN_KEY_CHUNK
    head = pl.BlockSpec((s, hw), lambda h: (0, h))
    vmem = 2 * 5 * s * hw * 2 + 4 * hw * tq * 4 + 10 * tk * tq * 4
    return pl.pallas_call(
        functools.partial(_attn_kernel, tq=tq, tk=tk, lam_init=lam_init),
        grid=(n_heads,),
        in_specs=[head, head,
                  pl.BlockSpec((s // tk, hw, tk), lambda h: (0, h, 0)),
                  head,
                  pl.BlockSpec((4, _HEAD_DIM), lambda h: (0, 0)),
                  pl.BlockSpec((hw, 1), lambda h: (0, 0))],
        out_specs=head,
        out_shape=jax.ShapeDtypeStruct((s, width), _BF16),
        scratch_shapes=[pltpu.VMEM((tk, tq), _F32),
                        pltpu.VMEM((tk, tq), _F32),
                        pltpu.VMEM((tk, tk), _F32),
                        pltpu.VMEM((2, 8, tq), _F32),
                        pltpu.VMEM((2, 2, hw, tq), _F32)],
        compiler_params=_params(("arbitrary",), vmem),
        name="diff_attention",
    )(q, k, vt, sz, lam_params, subln_g.reshape(hw, 1))


def kernel(x, c, positions, a_norm_g, a_ada_w, a_ada_b, a_w_in, a_w_group, a_scale, a_w_out,
           kv_norm_g, kv_ada_w, kv_ada_b, w_kv,
           b_norm_g, b_ada_w, b_ada_b, b_w_qg, b_lam, b_subln_g, b_w_out, final_g):
    batch, s, d = x.shape
    n_a = a_norm_g.shape[0]
    n_b = b_norm_g.shape[0]
    assert batch == 1 and n_a == 1 and n_b == 1
    pool_width = a_w_group.shape[1] * a_w_group.shape[2]
    group = a_w_group.shape[2]
    att_width = b_w_out.shape[1]
    assert a_w_group.shape[1] == len(_POOL_WINDOWS)

    h = x[0]
    tm, tn = _TILE_ROWS, _TILE_COLS_F32
    col = lambda j, i: (0, j)

    mod = _ada_mod(c, a_ada_w[0], a_ada_b[0])
    shift, scale, gate = mod[:, :d], mod[:, d:2 * d], mod[:, 2 * d:]
    hn5 = _norm_pool(h, a_norm_g[0], scale, shift)

    tn2 = _TILE_COLS_BF16
    per_group2 = group // tn2
    w_fold = _matmul(a_w_in[0], a_w_group[0], k=group, n=pool_width,
                     x_block=(tm, group), x_map=lambda j, i: (i, j // per_group2),
                     w_block=(None, group, tn2),
                     w_map=lambda j, i: (j // per_group2, 0, j % per_group2),
                     out_dtype=_BF16, epilogue=_ep_cast, tm=tm, tn=tn2, name="fold_group")
    wide_tile = pl.BlockSpec((tm, tn2), lambda j, i: (i, j))
    mixed, w_gate = _matmul(hn5, w_fold, k=d, n=pool_width,
                            x_block=(None, tm, d),
                            x_map=lambda j, i: (1 + j // per_group2, i, 0),
                            w_block=(d, tn2), w_map=col,
                            out_dtype=_BF16, epilogue=_ep_scale,
                            extra=(a_scale[0].reshape(1, pool_width),),
                            extra_specs=(pl.BlockSpec((1, tn2), col),),
                            tm=tm, tn=tn2, name="in_proj_pool",
                            side_cast=(a_w_in[0], pool_width, 1))
    y, w_out_a = _matmul(hn5, w_gate, k=d, n=pool_width,
                         x_block=(None, tm, d), x_map=lambda j, i: (0, i, 0),
                         w_block=(d, tn2), w_map=col,
                         out_dtype=_BF16, epilogue=_ep_silu_gate,
                         extra=(mixed,), extra_specs=(wide_tile,),
                         tm=tm, tn=tn2, name="in_proj_gate",
                         side_cast=(a_w_out[0], d, 0))
    tm3, tn3 = _RES_TILE_ROWS, _RES_TILE_COLS
    res_tile = pl.BlockSpec((tm3, tn3), lambda j, i: (i, j))
    res_vec = pl.BlockSpec((1, tn3), col)
    kv_halves = []
    for kh in range(pool_width // d):
        h, w_half = _matmul(y, w_out_a, k=d, n=d,
                            x_block=(tm3, d), x_map=lambda j, i, kh=kh: (i, kh),
                            w_block=(d, tn3), w_map=lambda j, i, kh=kh: (kh, j),
                            out_dtype=_F32, epilogue=_ep_residual,
                            extra=(h, gate), extra_specs=(res_tile, res_vec),
                            tm=tm3, tn=tn3, name=f"out_proj_a{kh}", w_single_buffer=True,
                            side_cast=(w_kv, att_width, kh))
        kv_halves.append(w_half)
    w_k_bf, w_v_bf = kv_halves

    kv_mod = _ada_mod(c, kv_ada_w, kv_ada_b)
    kv_shift, kv_scale = kv_mod[:, :d], kv_mod[:, d:]
    b_mod = _ada_mod(c, b_ada_w[0], b_ada_b[0])
    b_shift, b_scale, b_gate = b_mod[:, :d], b_mod[:, d:2 * d], b_mod[:, 2 * d:]
    kvn, hnb = _norm2(h, kv_norm_g, kv_scale, kv_shift, b_norm_g[0], b_scale, b_shift)

    cos2, sin2 = _rope_tables(positions)
    rope_specs = (pl.BlockSpec((tm, _HEAD_DIM), lambda j, i: (i, 0)),) * 2
    flat = dict(k=d, n=att_width, x_block=(tm, d), x_map=lambda j, i: (i, 0), tm=tm)
    wide = dict(w_block=(d, tn2), tn=tn2, **flat)
    k_rot, w_q_bf = _matmul(kvn, w_k_bf, w_map=col, out_dtype=_BF16,
                            epilogue=functools.partial(_ep_rotary, mult=1.0),
                            extra=(cos2, sin2), extra_specs=rope_specs, name="k_proj",
                            side_cast=(b_w_qg[0], att_width, 0), **wide)
    tk = _ATTN_KEY_CHUNK
    v_t, w_g_bf = _matmul(kvn, w_v_bf, w_map=col, out_dtype=_BF16,
                          epilogue=_ep_transpose, name="v_proj",
                          out_spec=pl.BlockSpec((tm // tk, tn2, tk), lambda j, i: (i, j, 0)),
                          out_shape=(s // tk, att_width, tk),
                          side_cast=(b_w_qg[0], att_width, 1), **wide)
    q_rot, w_out_b = _matmul(hnb, w_q_bf, w_map=col, out_dtype=_BF16,
                             epilogue=functools.partial(
                                 _ep_rotary, mult=_HEAD_DIM ** -0.5 * math.log2(math.e)),
                             extra=(cos2, sin2), extra_specs=rope_specs, name="q_proj",
                             side_cast=(b_w_out[0], d, 0), **wide)
    szb = _matmul(hnb, w_g_bf, w_map=col, out_dtype=_BF16,
                  epilogue=_ep_silu, name="g_proj", **wide)

    layer = n_a
    lam_init = 0.8 - 0.6 * math.exp(-0.3 * layer)
    yb = _attention(q_rot, k_rot, v_t, szb, b_lam[0], b_subln_g[0], lam_init=lam_init)
    h = _matmul(yb, w_out_b, k=att_width, n=d,
                x_block=(tm3, att_width), x_map=lambda j, i: (i, 0),
                w_block=(att_width, tn3), w_map=col,
                out_dtype=_F32, epilogue=_ep_residual,
                extra=(h, b_gate), extra_specs=(res_tile, res_vec),
                tm=tm3, tn=tn3, name="out_proj_b", w_single_buffer=True)

    return _final_norm(h, final_g)[None]
```

```python
import functools
import math

import jax
import jax.numpy as jnp
from jax import lax
from jax.experimental import pallas as pl
from jax.experimental.pallas import tpu as pltpu

_EPS = 1e-6
_POOL_WINDOWS = (2, 4, 8, 16)
_MAX_WINDOW = max(_POOL_WINDOWS)
_HEAD_DIM = 128
_CHUNK = 64
_ROPE_THETA = 10000.0

_V7X_LANES = 128
_V7X_VMEM_BYTES = 64 * 1024 * 1024
_V7X_VMEM_BUDGET = _V7X_VMEM_BYTES - 8 * 1024 * 1024

_BF16 = jnp.bfloat16
_F32 = jnp.float32


def _params(semantics, vmem_bytes):
    return pltpu.CompilerParams(
        dimension_semantics=semantics,
        vmem_limit_bytes=int(min(max(vmem_bytes, 16 * 1024 * 1024), _V7X_VMEM_BUDGET)),
    )


def _silu(x):
    return x * (1.0 / (1.0 + jnp.exp(-x)))


def _ada_kernel(c_ref, w_ref, b_ref, o_ref):
    sc = _silu(c_ref[...])
    lhs = jnp.broadcast_to(sc, (8, sc.shape[1]))
    acc = jnp.dot(lhs, w_ref[...], preferred_element_type=_F32)
    o_ref[...] = acc[0:1, :] + b_ref[...]


def _ada_mod(c, w, b):
    d, n = w.shape
    tn = 1024
    return pl.pallas_call(
        _ada_kernel,
        grid=(n // tn,),
        in_specs=[
            pl.BlockSpec((1, d), lambda j: (0, 0)),
            pl.BlockSpec((d, tn), lambda j: (0, j)),
            pl.BlockSpec((1, tn), lambda j: (0, j)),
        ],
        out_specs=pl.BlockSpec((1, tn), lambda j: (0, j)),
        out_shape=jax.ShapeDtypeStruct((1, n), _F32),
        compiler_params=_params(("arbitrary",), 3 * d * tn * 4),
        name="ada_mod",
    )(c, w, b.reshape(1, n))


_ROW_GROUP = 16
_COL_STRIP = 1024


def _for_row_groups(x_ref, n_rows, emit):
    d = x_ref.shape[-1]

    def body(r, carry):
        r0 = pl.multiple_of(r * _ROW_GROUP, _ROW_GROUP)
        x = x_ref[pl.ds(r0, _ROW_GROUP), :]
        rs = lax.rsqrt(jnp.mean(x * x, axis=-1, keepdims=True) + _EPS)
        for cc in range(d // _COL_STRIP):
            strip = slice(cc * _COL_STRIP, (cc + 1) * _COL_STRIP)
            emit(r0, strip, x_ref[pl.ds(r0, _ROW_GROUP), strip] * rs)
        return carry

    lax.fori_loop(0, n_rows // _ROW_GROUP, body, 0, unroll=2)


def _norm_pool_kernel(x_ref, g_ref, sc_ref, sh_ref, o_ref, ext_ref, coef_ref, *, ts):
    i = pl.program_id(0)
    m = _MAX_WINDOW
    coef_ref[0:1, :] = g_ref[...] * (1.0 + sc_ref[...])

    @pl.when(i == 0)
    def _():
        ext_ref[0:m, :] = jnp.zeros((m, ext_ref.shape[1]), _F32)

    @pl.when(i != 0)
    def _():
        ext_ref[0:m, :] = ext_ref[ts:ts + m, :]

    def emit(r0, strip, xn):
        hn = xn * coef_ref[0:1, strip] + sh_ref[:, strip]
        ext_ref[pl.ds(pl.multiple_of(m + r0, _ROW_GROUP), _ROW_GROUP), strip] = hn
        o_ref[0, pl.ds(r0, _ROW_GROUP), strip] = hn.astype(_BF16)

    _for_row_groups(x_ref, ts, emit)

    n_win = len(_POOL_WINDOWS)
    row = lax.broadcasted_iota(jnp.int32, (n_win * ts, ts + m), 0)
    col = lax.broadcasted_iota(jnp.int32, (n_win * ts, ts + m), 1)
    last = row % ts + m
    width = jnp.zeros_like(row)
    for gidx, w in enumerate(_POOL_WINDOWS):
        width = jnp.where(row // ts == gidx, w, width)
    band = jnp.where((col <= last) & (col > last - width), 1.0, 0.0).astype(_BF16)
    wsum = jnp.dot(band, ext_ref[...].astype(_BF16), preferred_element_type=_F32)

    t = i * ts + lax.broadcasted_iota(jnp.int32, (ts, 1), 0)
    for gidx, w in enumerate(_POOL_WINDOWS):
        inv_cnt = 1.0 / jnp.minimum(t + 1, w).astype(_F32)
        o_ref[1 + gidx] = (wsum[gidx * ts:(gidx + 1) * ts, :] * inv_cnt
                           - ext_ref[m:m + ts, :]).astype(_BF16)


def _norm_pool(x, g, scale, shift):
    s, d = x.shape
    ts = 128
    n_out = 1 + len(_POOL_WINDOWS)
    vec = pl.BlockSpec((1, d), lambda i: (0, 0))
    return pl.pallas_call(
        functools.partial(_norm_pool_kernel, ts=ts),
        grid=(s // ts,),
        in_specs=[pl.BlockSpec((ts, d), lambda i: (i, 0)), vec, vec, vec],
        out_specs=pl.BlockSpec((n_out, ts, d), lambda i: (0, i, 0)),
        out_shape=jax.ShapeDtypeStruct((n_out, s, d), _BF16),
        scratch_shapes=[pltpu.VMEM((ts + _MAX_WINDOW, d), _F32), pltpu.VMEM((8, d), _F32)],
        compiler_params=_params(("arbitrary",), 2 * ts * d * (4 + 2 * n_out) + 12 * ts * d * 4),
        name="norm_pool",
    )(x, g.reshape(1, d), scale, shift)


def _norm2_kernel(x_ref, g1_ref, sc1_ref, sh1_ref, g2_ref, sc2_ref, sh2_ref, o1_ref, o2_ref):
    x = x_ref[...]
    ms = jnp.mean(x * x, axis=-1, keepdims=True)
    xn = x * lax.rsqrt(ms + _EPS)
    o1_ref[...] = (xn * g1_ref[...] * (1.0 + sc1_ref[...]) + sh1_ref[...]).astype(_BF16)
    o2_ref[...] = (xn * g2_ref[...] * (1.0 + sc2_ref[...]) + sh2_ref[...]).astype(_BF16)


def _norm2(x, g1, sc1, sh1, g2, sc2, sh2):
    s, d = x.shape
    ts = 256
    vec = pl.BlockSpec((1, d), lambda i: (0, 0))
    row = pl.BlockSpec((ts, d), lambda i: (i, 0))
    return pl.pallas_call(
        _norm2_kernel,
        grid=(s // ts,),
        in_specs=[row, vec, vec, vec, vec, vec, vec],
        out_specs=[row, row],
        out_shape=[jax.ShapeDtypeStruct((s, d), _BF16)] * 2,
        compiler_params=_params(("arbitrary",), 2 * ts * d * 8 + 6 * ts * d * 4),
        name="norm2",
    )(x, g1.reshape(1, d), sc1, sh1, g2.reshape(1, d), sc2, sh2)


def _final_norm_kernel(x_ref, g_ref, o_ref):
    x = x_ref[...]
    ms = jnp.mean(x * x, axis=-1, keepdims=True)
    o_ref[...] = x * lax.rsqrt(ms + _EPS) * g_ref[...]


def _final_norm(x, g):
    s, d = x.shape
    ts = 256
    row = pl.BlockSpec((ts, d), lambda i: (i, 0))
    return pl.pallas_call(
        _final_norm_kernel,
        grid=(s // ts,),
        in_specs=[row, pl.BlockSpec((1, d), lambda i: (0, 0))],
        out_specs=row,
        out_shape=jax.ShapeDtypeStruct((s, d), _F32),
        compiler_params=_params(("arbitrary",), 4 * ts * d * 4 + 4 * ts * d * 4),
        name="final_norm",
    )(x, g.reshape(1, d))


def _rope_kernel(pos_ref, cos_ref, sin_ref):
    half = _HEAD_DIM // 2
    lane = lax.broadcasted_iota(jnp.int32, (1, _HEAD_DIM), 1)
    f = jnp.where(lane < half, lane, lane - half).astype(_F32)
    inv = jnp.exp(f * (-2.0 * math.log(_ROPE_THETA) / _HEAD_DIM))
    ang = pos_ref[...].astype(_F32) * inv
    cos_ref[...] = jnp.cos(ang)
    sin_ref[...] = jnp.where(lane < half, -1.0, 1.0) * jnp.sin(ang)


def _rope_tables(positions):
    s = positions.shape[-1]
    ts = 1024
    tab = pl.BlockSpec((ts, _HEAD_DIM), lambda i: (i, 0))
    return pl.pallas_call(
        _rope_kernel,
        grid=(s // ts,),
        in_specs=[pl.BlockSpec((ts, 1), lambda i: (i, 0))],
        out_specs=[tab, tab],
        out_shape=[jax.ShapeDtypeStruct((s, _HEAD_DIM), _F32)] * 2,
        compiler_params=_params(("arbitrary",), 16 * ts * _HEAD_DIM * 4),
        name="rope_tables",
    )(positions.reshape(s, 1))


_MM_SUB_ROWS = 512

_TILE_ROWS = 1024
_TILE_COLS_F32 = 512
_TILE_COLS_BF16 = 1024
_RES_TILE_ROWS = 512
_RES_TILE_COLS = 2048


def _mm_kernel(x_ref, w_ref, *rest, epilogue, n_extra, cast_w, side, n_split):
    rest = list(rest)
    extra = [rest.pop(0) for _ in range(n_extra)]
    side_in = rest.pop(0) if side else None
    o_ref = rest.pop(0)
    side_out = rest.pop(0) if side else None
    if cast_w:
        wbf_ref = rest.pop(0)

        @pl.when(pl.program_id(1) == 0)
        def _():
            wbf_ref[...] = w_ref[...].astype(_BF16)

        w = wbf_ref[...]
    else:
        w = w_ref[...]
    sub = x_ref.shape[0] // n_split
    for r in range(n_split):
        rows = slice(r * sub, (r + 1) * sub)
        acc = jnp.dot(x_ref[rows, :].astype(_BF16), w, preferred_element_type=_F32)
        epilogue(acc, extra, o_ref, rows)
    if side:
        side_out[...] = side_in[...].astype(_BF16)


def _ep_cast(acc, extra, o_ref, rows):
    o_ref[rows, :] = acc.astype(o_ref.dtype)


def _ep_silu(acc, extra, o_ref, rows):
    o_ref[rows, :] = _silu(acc).astype(o_ref.dtype)


def _ep_scale(acc, extra, o_ref, rows):
    (scale_ref,) = extra
    o_ref[rows, :] = (acc * scale_ref[...]).astype(o_ref.dtype)


def _ep_silu_gate(acc, extra, o_ref, rows):
    (mixed_ref,) = extra
    o_ref[rows, :] = (_silu(acc) * mixed_ref[rows, :].astype(_F32)).astype(o_ref.dtype)


def _ep_residual(acc, extra, o_ref, rows):
    res_ref, gate_ref = extra
    o_ref[rows, :] = res_ref[rows, :] + gate_ref[...] * acc


def _ep_rotary(acc, extra, o_ref, rows, *, mult):
    cos_ref, sin_ref = extra
    cos = cos_ref[rows, :]
    sin = sin_ref[rows, :]
    for c in range(acc.shape[1] // _HEAD_DIM):
        lanes = slice(c * _HEAD_DIM, (c + 1) * _HEAD_DIM)
        a = acc[:, lanes]
        rot = a * cos + pltpu.roll(a, _HEAD_DIM // 2, axis=1) * sin
        if mult != 1.0:
            rot = rot * mult
        o_ref[rows, lanes] = rot.astype(o_ref.dtype)


def _ep_transpose(acc, extra, o_ref, rows):
    chunk = o_ref.shape[2]
    n = acc.shape[0]
    first, offset = divmod(rows.start, chunk)
    if n < chunk:
        o_ref[first, :, offset:offset + n] = acc.T.astype(o_ref.dtype)
    else:
        for r in range(n // chunk):
            o_ref[first + r] = acc[r * chunk:(r + 1) * chunk, :].T.astype(o_ref.dtype)


def _matmul(x, w, *, k, n, x_map, w_map, w_block, x_block, out_dtype, epilogue,
            extra=(), extra_specs=(), tm, tn, name, out_spec=None, out_shape=None,
            side_cast=None, w_single_buffer=False):
    m = x.shape[-2]
    cast_w = w.dtype != _BF16
    n_i = m // tm
    steps = (n // tn) * n_i
    out_bytes = jnp.dtype(out_dtype).itemsize
    w_bufs = 1 if w_single_buffer else 2
    vmem = (2 * tm * k * x.dtype.itemsize + w_bufs * k * tn * w.dtype.itemsize
            + (k * tn * 2 if cast_w else 0) + 2 * tm * tn * out_bytes
            + 4 * tm * tn * 4 + sum(2 * tm * tn * 4 for _ in extra))
    if out_spec is None:
        out_spec = pl.BlockSpec((tm, tn), lambda j, i: (i, j))
        out_shape = (m, n)
    w_spec = (pl.BlockSpec(w_block, w_map, pipeline_mode=pl.Buffered(1)) if w_single_buffer
              else pl.BlockSpec(w_block, w_map))
    in_specs = [pl.BlockSpec(x_block, x_map), w_spec, *extra_specs]
    out_specs = [out_spec]
    out_shapes = [jax.ShapeDtypeStruct(out_shape, out_dtype)]
    operands = [x, w, *extra]
    if side_cast is not None:
        src, width, col_block = side_cast
        rows = src.shape[0] // steps
        assert rows * steps == src.shape[0] and rows % 16 == 0
        slab = pl.BlockSpec((rows, width), lambda j, i: (j * n_i + i, col_block))
        in_specs.append(slab)
        operands.append(src)
        out_specs.append(pl.BlockSpec((rows, width), lambda j, i: (j * n_i + i, 0)))
        out_shapes.append(jax.ShapeDtypeStruct((src.shape[0], width), _BF16))
        vmem += 2 * rows * width * 6
    res = pl.pallas_call(
        functools.partial(_mm_kernel, epilogue=epilogue, n_extra=len(extra), cast_w=cast_w,
                          side=side_cast is not None, n_split=max(1, tm // _MM_SUB_ROWS)),
        grid=(n // tn, n_i),
        in_specs=in_specs,
        out_specs=out_specs,
        out_shape=out_shapes,
        scratch_shapes=[pltpu.VMEM((k, tn), _BF16)] if cast_w else [],
        compiler_params=_params(("arbitrary", "arbitrary"), vmem),
        name=name,
    )(*operands)
    return res if side_cast is not None else res[0]


def _attn_kernel(q_ref, k_ref, vt_ref, sz_ref, lam_ref, g_ref, o_ref,
                 s0_ref, s1_ref, bias_ref, stat_ref, acc_ref, *, tq, tk, lam_init):
    dh = _HEAD_DIM
    s_refs = (s0_ref, s1_ref)

    @pl.when(pl.program_id(0) == 0)
    def _():
        key_chunk = lax.broadcasted_iota(jnp.int32, (tk, tk), 0) // _CHUNK
        qry_chunk = lax.broadcasted_iota(jnp.int32, (tk, tk), 1) // _CHUNK
        bias_ref[...] = jnp.where(key_chunk <= qry_chunk, 0.0, -jnp.inf)

    lp = lam_ref[...]
    lam = (jnp.exp(jnp.sum(lp[0:1] * lp[1:2], axis=1, keepdims=True))
           - jnp.exp(jnp.sum(lp[2:3] * lp[3:4], axis=1, keepdims=True)) + lam_init)
    g_col = g_ref[...] * (1.0 - lam_init)
    every = slice(0, tq)
    late = slice(tk, tq)

    def tile(i):
        q0 = i * tq
        slot = i % 2
        stat_ref[slot, 0:2, :] = jnp.full((2, tq), -jnp.inf, _F32)
        stat_ref[slot, 2:4, :] = jnp.zeros((2, tq), _F32)
        acc_ref[slot] = jnp.zeros(acc_ref.shape[1:], _F32)

        def scores(chunk, c, qs, diagonal):
            start = chunk * tk if isinstance(chunk, int) else pl.multiple_of(chunk * tk, tk)
            st = lax.dot_general(k_ref[pl.ds(start, tk), c * dh:(c + 1) * dh],
                                 q_ref[q0 + qs.start:q0 + qs.stop, c * dh:(c + 1) * dh],
                                 (((1,), (1,)), ((), ())),
                                 preferred_element_type=_F32)
            if diagonal:
                blk = st[:, :tk] + bias_ref[...]
                st = blk if st.shape[1] == tk else jnp.concatenate([blk, st[:, tk:]], axis=1)
            s_refs[c][:, qs] = st
            stat_ref[slot, 4 + c:5 + c, qs] = jnp.max(st, axis=0, keepdims=True)

        def update(chunk, c, qs):
            st = s_refs[c][:, qs]
            m_prev = stat_ref[slot, c:c + 1, qs]
            m_new = jnp.maximum(m_prev, stat_ref[slot, 4 + c:5 + c, qs])
            alpha = jnp.exp2(m_prev - m_new)
            pt = jnp.exp2(st - m_new)
            stat_ref[slot, 2 + c:3 + c, qs] = (alpha * stat_ref[slot, 2 + c:3 + c, qs]
                                               + jnp.sum(pt, axis=0, keepdims=True))
            stat_ref[slot, c:c + 1, qs] = m_new
            pv = jnp.dot(vt_ref[chunk], pt.astype(_BF16), preferred_element_type=_F32)
            acc_ref[slot, c, :, qs] = alpha * acc_ref[slot, c, :, qs] + pv

        diag_a, diag_b = 2 * i, 2 * i + 1
        scores(diag_b, 0, late, True)
        scores(diag_b, 1, late, True)
        update(diag_b, 0, late)
        scores(diag_a, 0, every, True)
        update(diag_b, 1, late)
        scores(diag_a, 1, every, True)
        update(diag_a, 0, every)
        if i > 0:
            scores(0, 0, every, False)
        update(diag_a, 1, every)

        def pair(p, look_ahead):
            j0, j1 = 2 * p, 2 * p + 1
            scores(j0, 1, every, False)
            update(j0, 0, every)
            scores(j1, 0, every, False)
            update(j0, 1, every)
            scores(j1, 1, every, False)
            update(j1, 0, every)
            if look_ahead:
                scores(j1 + 1, 0, every, False)
            update(j1, 1, every)

        if i > 1:
            def body(p, carry):
                pair(p, True)
                return carry
            lax.fori_loop(0, i - 1, body, 0)
        if i > 0:
            pair(i - 1, False)

        inv1 = 1.0 / stat_ref[slot, 2:3, :]
        inv2 = lam / stat_ref[slot, 3:4, :]
        ot = acc_ref[slot, 0] * inv1 - acc_ref[slot, 1] * inv2
        ms = jnp.mean(ot * ot, axis=0, keepdims=True)
        ont = ot * lax.rsqrt(ms + _EPS) * g_col
        rows = slice(q0, q0 + tq)
        o_ref[rows, :] = (ont.T * sz_ref[rows, :].astype(_F32)).astype(o_ref.dtype)

    for i in range(q_ref.shape[0] // tq):
        tile(i)


_ATTN_KEY_CHUNK = 512
_ATTN_QUERY_TILE = 2 * _ATTN_KEY_CHUNK


def _attention(q, k, vt, sz, lam_params, subln_g, *, lam_init):
    s, width = q.shape
    hw = 2 * _HEAD_DIM
    n_heads = width // hw
    tq, tk = _ATTN_QUERY_TILE, _ATTN_KEY_CHUNK
    head = pl.BlockSpec((s, hw), lambda h: (0, h))
    vmem = 2 * 5 * s * hw * 2 + 4 * hw * tq * 4 + 10 * tk * tq * 4
    return pl.pallas_call(
        functools.partial(_attn_kernel, tq=tq, tk=tk, lam_init=lam_init),
        grid=(n_heads,),
        in_specs=[head, head,
                  pl.BlockSpec((s // tk, hw, tk), lambda h: (0, h, 0)),
                  head,
                  pl.BlockSpec((4, _HEAD_DIM), lambda h: (0, 0)),
                  pl.BlockSpec((hw, 1), lambda h: (0, 0))],
        out_specs=head,
        out_shape=jax.ShapeDtypeStruct((s, width), _BF16),
        scratch_shapes=[pltpu.VMEM((tk, tq), _F32),
                        pltpu.VMEM((tk, tq), _F32),
                        pltpu.VMEM((tk, tk), _F32),
                        pltpu.VMEM((2, 8, tq), _F32),
                        pltpu.VMEM((2, 2, hw, tq), _F32)],
        compiler_params=_params(("arbitrary",), vmem),
        name="diff_attention",
    )(q, k, vt, sz, lam_params, subln_g.reshape(hw, 1))


def kernel(x, c, positions, a_norm_g, a_ada_w, a_ada_b, a_w_in, a_w_group, a_scale, a_w_out,
           kv_norm_g, kv_ada_w, kv_ada_b, w_kv,
           b_norm_g, b_ada_w, b_ada_b, b_w_qg, b_lam, b_subln_g, b_w_out, final_g):
    batch, s, d = x.shape
    n_a = a_norm_g.shape[0]
    n_b = b_norm_g.shape[0]
    assert batch == 1 and n_a == 1 and n_b == 1
    pool_width = a_w_group.shape[1] * a_w_group.shape[2]
    group = a_w_group.shape[2]
    att_width = b_w_out.shape[1]
    assert a_w_group.shape[1] == len(_POOL_WINDOWS)

    h = x[0]
    tm, tn = _TILE_ROWS, _TILE_COLS_F32
    col = lambda j, i: (0, j)

    mod = _ada_mod(c, a_ada_w[0], a_ada_b[0])
    shift, scale, gate = mod[:, :d], mod[:, d:2 * d], mod[:, 2 * d:]
    hn5 = _norm_pool(h, a_norm_g[0], scale, shift)

    tn2 = _TILE_COLS_BF16
    per_group2 = group // tn2
    w_fold = _matmul(a_w_in[0], a_w_group[0], k=group, n=pool_width,
                     x_block=(tm, group), x_map=lambda j, i: (i, j // per_group2),
                     w_block=(None, group, tn2),
                     w_map=lambda j, i: (j // per_group2, 0, j % per_group2),
                     out_dtype=_BF16, epilogue=_ep_cast, tm=tm, tn=tn2, name="fold_group")
    wide_tile = pl.BlockSpec((tm, tn2), lambda j, i: (i, j))
    mixed, w_gate = _matmul(hn5, w_fold, k=d, n=pool_width,
                            x_block=(None, tm, d),
                            x_map=lambda j, i: (1 + j // per_group2, i, 0),
                            w_block=(d, tn2), w_map=col,
                            out_dtype=_BF16, epilogue=_ep_scale,
                            extra=(a_scale[0].reshape(1, pool_width),),
                            extra_specs=(pl.BlockSpec((1, tn2), col),),
                            tm=tm, tn=tn2, name="in_proj_pool",
                            side_cast=(a_w_in[0], pool_width, 1))
    y, w_out_a = _matmul(hn5, w_gate, k=d, n=pool_width,
                         x_block=(None, tm, d), x_map=lambda j, i: (0, i, 0),
                         w_block=(d, tn2), w_map=col,
                         out_dtype=_BF16, epilogue=_ep_silu_gate,
                         extra=(mixed,), extra_specs=(wide_tile,),
                         tm=tm, tn=tn2, name="in_proj_gate",
                         side_cast=(a_w_out[0], d, 0))
    tm3, tn3 = _RES_TILE_ROWS, _RES_TILE_COLS
    res_tile = pl.BlockSpec((tm3, tn3), lambda j, i: (i, j))
    res_vec = pl.BlockSpec((1, tn3), col)
    kv_halves = []
    for kh in range(pool_width // d):
        h, w_half = _matmul(y, w_out_a, k=d, n=d,
                            x_block=(tm3, d), x_map=lambda j, i, kh=kh: (i, kh),
                            w_block=(d, tn3), w_map=lambda j, i, kh=kh: (kh, j),
                            out_dtype=_F32, epilogue=_ep_residual,
                            extra=(h, gate), extra_specs=(res_tile, res_vec),
                            tm=tm3, tn=tn3, name=f"out_proj_a{kh}", w_single_buffer=True,
                            side_cast=(w_kv, att_width, kh))
        kv_halves.append(w_half)
    w_k_bf, w_v_bf = kv_halves

    kv_mod = _ada_mod(c, kv_ada_w, kv_ada_b)
    kv_shift, kv_scale = kv_mod[:, :d], kv_mod[:, d:]
    b_mod = _ada_mod(c, b_ada_w[0], b_ada_b[0])
    b_shift, b_scale, b_gate = b_mod[:, :d], b_mod[:, d:2 * d], b_mod[:, 2 * d:]
    kvn, hnb = _norm2(h, kv_norm_g, kv_scale, kv_shift, b_norm_g[0], b_scale, b_shift)

    cos2, sin2 = _rope_tables(positions)
    rope_specs = (pl.BlockSpec((tm, _HEAD_DIM), lambda j, i: (i, 0)),) * 2
    flat = dict(k=d, n=att_width, x_block=(tm, d), x_map=lambda j, i: (i, 0), tm=tm)
    wide = dict(w_block=(d, tn2), tn=tn2, **flat)
    k_rot, w_q_bf = _matmul(kvn, w_k_bf, w_map=col, out_dtype=_BF16,
                            epilogue=functools.partial(_ep_rotary, mult=1.0),
                            extra=(cos2, sin2), extra_specs=rope_specs, name="k_proj",
                            side_cast=(b_w_qg[0], att_width, 0), **wide)
    tk = _ATTN_KEY_CHUNK
    v_t, w_g_bf = _matmul(kvn, w_v_bf, w_map=col, out_dtype=_BF16,
                          epilogue=_ep_transpose, name="v_proj",
                          out_spec=pl.BlockSpec((tm // tk, tn2, tk), lambda j, i: (i, j, 0)),
                          out_shape=(s // tk, att_width, tk),
                          side_cast=(b_w_qg[0], att_width, 1), **wide)
    q_rot, w_out_b = _matmul(hnb, w_q_bf, w_map=col, out_dtype=_BF16,
                             epilogue=functools.partial(
                                 _ep_rotary, mult=_HEAD_DIM ** -0.5 * math.log2(math.e)),
                             extra=(cos2, sin2), extra_specs=rope_specs, name="q_proj",
                             side_cast=(b_w_out[0], d, 0), **wide)
    szb = _matmul(hnb, w_g_bf, w_map=col, out_dtype=_BF16,
                  epilogue=_ep_silu, name="g_proj", **wide)

    layer = n_a
    lam_init = 0.8 - 0.6 * math.exp(-0.3 * layer)
    yb = _attention(q_rot, k_rot, v_t, szb, b_lam[0], b_subln_g[0], lam_init=lam_init)
    h = _matmul(yb, w_out_b, k=att_width, n=d,
                x_block=(tm3, att_width), x_map=lambda j, i: (i, 0),
                w_block=(att_width, tn3), w_map=col,
                out_dtype=_F32, epilogue=_ep_residual,
                extra=(h, b_gate), extra_specs=(res_tile, res_vec),
                tm=tm3, tn=tn3, name="out_proj_b", w_single_buffer=True)

    return _final_norm(h, final_g)[None]
```

```python
import functools
import math

import jax
import jax.numpy as jnp
from jax import lax
from jax.experimental import pallas as pl
from jax.experimental.pallas import tpu as pltpu

_EPS = 1e-6
_POOL_WINDOWS = (2, 4, 8, 16)
_MAX_WINDOW = max(_POOL_WINDOWS)
_HEAD_DIM = 128
_CHUNK = 64
_ROPE_THETA = 10000.0

_V7X_LANES = 128
_V7X_VMEM_BYTES = 64 * 1024 * 1024
_V7X_VMEM_BUDGET = _V7X_VMEM_BYTES - 8 * 1024 * 1024

_BF16 = jnp.bfloat16
_F32 = jnp.float32


def _params(semantics, vmem_bytes):
    return pltpu.CompilerParams(
        dimension_semantics=semantics,
        vmem_limit_bytes=int(min(max(vmem_bytes, 16 * 1024 * 1024), _V7X_VMEM_BUDGET)),
    )


def _silu(x):
    return x * (1.0 / (1.0 + jnp.exp(-x)))


def _ada_kernel(c_ref, w_ref, b_ref, o_ref):
    sc = _silu(c_ref[...])
    lhs = jnp.broadcast_to(sc, (8, sc.shape[1]))
    acc = jnp.dot(lhs, w_ref[...], preferred_element_type=_F32)
    o_ref[...] = acc[0:1, :] + b_ref[...]


def _ada_mod(c, w, b):
    d, n = w.shape
    tn = 512
    return pl.pallas_call(
        _ada_kernel,
        grid=(n // tn,),
        in_specs=[
            pl.BlockSpec((1, d), lambda j: (0, 0)),
            pl.BlockSpec((d, tn), lambda j: (0, j)),
            pl.BlockSpec((1, tn), lambda j: (0, j)),
        ],
        out_specs=pl.BlockSpec((1, tn), lambda j: (0, j)),
        out_shape=jax.ShapeDtypeStruct((1, n), _F32),
        compiler_params=_params(("arbitrary",), 3 * d * tn * 4),
        name="ada_mod",
    )(c, w, b.reshape(1, n))


_ROW_GROUP = 16
_COL_STRIP = 1024


def _for_row_groups(x_ref, n_rows, emit):
    d = x_ref.shape[-1]

    def body(r, carry):
        r0 = pl.multiple_of(r * _ROW_GROUP, _ROW_GROUP)
        x = x_ref[pl.ds(r0, _ROW_GROUP), :]
        rs = lax.rsqrt(jnp.mean(x * x, axis=-1, keepdims=True) + _EPS)
        for cc in range(d // _COL_STRIP):
            strip = slice(cc * _COL_STRIP, (cc + 1) * _COL_STRIP)
            emit(r0, strip, x_ref[pl.ds(r0, _ROW_GROUP), strip] * rs)
        return carry

    lax.fori_loop(0, n_rows // _ROW_GROUP, body, 0, unroll=2)


def _norm_pool_kernel(x_ref, g_ref, sc_ref, sh_ref, o_ref, ext_ref, coef_ref, *, ts):
    i = pl.program_id(0)
    m = _MAX_WINDOW
    coef_ref[0:1, :] = g_ref[...] * (1.0 + sc_ref[...])

    @pl.when(i == 0)
    def _():
        ext_ref[0:m, :] = jnp.zeros((m, ext_ref.shape[1]), _F32)

    @pl.when(i != 0)
    def _():
        ext_ref[0:m, :] = ext_ref[ts:ts + m, :]

    def emit(r0, strip, xn):
        hn = xn * coef_ref[0:1, strip] + sh_ref[:, strip]
        ext_ref[pl.ds(pl.multiple_of(m + r0, _ROW_GROUP), _ROW_GROUP), strip] = hn
        o_ref[0, pl.ds(r0, _ROW_GROUP), strip] = hn.astype(_BF16)

    _for_row_groups(x_ref, ts, emit)

    n_win = len(_POOL_WINDOWS)
    row = lax.broadcasted_iota(jnp.int32, (n_win * ts, ts + m), 0)
    col = lax.broadcasted_iota(jnp.int32, (n_win * ts, ts + m), 1)
    last = row % ts + m
    width = jnp.zeros_like(row)
    for gidx, w in enumerate(_POOL_WINDOWS):
        width = jnp.where(row // ts == gidx, w, width)
    band = jnp.where((col <= last) & (col > last - width), 1.0, 0.0).astype(_BF16)
    wsum = jnp.dot(band, ext_ref[...].astype(_BF16), preferred_element_type=_F32)

    t = i * ts + lax.broadcasted_iota(jnp.int32, (ts, 1), 0)
    for gidx, w in enumerate(_POOL_WINDOWS):
        inv_cnt = 1.0 / jnp.minimum(t + 1, w).astype(_F32)
        o_ref[1 + gidx] = (wsum[gidx * ts:(gidx + 1) * ts, :] * inv_cnt
                           - ext_ref[m:m + ts, :]).astype(_BF16)


def _norm_pool(x, g, scale, shift):
    s, d = x.shape
    ts = 128
    n_out = 1 + len(_POOL_WINDOWS)
    vec = pl.BlockSpec((1, d), lambda i: (0, 0))
    return pl.pallas_call(
        functools.partial(_norm_pool_kernel, ts=ts),
        grid=(s // ts,),
        in_specs=[pl.BlockSpec((ts, d), lambda i: (i, 0)), vec, vec, vec],
        out_specs=pl.BlockSpec((n_out, ts, d), lambda i: (0, i, 0)),
        out_shape=jax.ShapeDtypeStruct((n_out, s, d), _BF16),
        scratch_shapes=[pltpu.VMEM((ts + _MAX_WINDOW, d), _F32), pltpu.VMEM((8, d), _F32)],
        compiler_params=_params(("arbitrary",), 2 * ts * d * (4 + 2 * n_out) + 12 * ts * d * 4),
        name="norm_pool",
    )(x, g.reshape(1, d), scale, shift)


def _norm2_kernel(x_ref, g1_ref, sc1_ref, sh1_ref, g2_ref, sc2_ref, sh2_ref, o1_ref, o2_ref,
                  coef_ref):
    coef_ref[0:1, :] = g1_ref[...] * (1.0 + sc1_ref[...])
    coef_ref[1:2, :] = g2_ref[...] * (1.0 + sc2_ref[...])

    def emit(r0, strip, xn):
        rows = pl.ds(r0, _ROW_GROUP)
        o1_ref[rows, strip] = (xn * coef_ref[0:1, strip] + sh1_ref[:, strip]).astype(_BF16)
        o2_ref[rows, strip] = (xn * coef_ref[1:2, strip] + sh2_ref[:, strip]).astype(_BF16)

    _for_row_groups(x_ref, x_ref.shape[0], emit)


def _norm2(x, g1, sc1, sh1, g2, sc2, sh2):
    s, d = x.shape
    ts = 256
    vec = pl.BlockSpec((1, d), lambda i: (0, 0))
    row = pl.BlockSpec((ts, d), lambda i: (i, 0))
    return pl.pallas_call(
        _norm2_kernel,
        grid=(s // ts,),
        in_specs=[row, vec, vec, vec, vec, vec, vec],
        out_specs=[row, row],
        out_shape=[jax.ShapeDtypeStruct((s, d), _BF16)] * 2,
        scratch_shapes=[pltpu.VMEM((8, d), _F32)],
        compiler_params=_params(("arbitrary",), 2 * ts * d * 8 + 6 * ts * d * 4),
        name="norm2",
    )(x, g1.reshape(1, d), sc1, sh1, g2.reshape(1, d), sc2, sh2)


def _final_norm_kernel(x_ref, g_ref, o_ref):
    def emit(r0, strip, xn):
        o_ref[pl.ds(r0, _ROW_GROUP), strip] = xn * g_ref[:, strip]

    _for_row_groups(x_ref, x_ref.shape[0], emit)


def _final_norm(x, g):
    s, d = x.shape
    ts = 256
    row = pl.BlockSpec((ts, d), lambda i: (i, 0))
    return pl.pallas_call(
        _final_norm_kernel,
        grid=(s // ts,),
        in_specs=[row, pl.BlockSpec((1, d), lambda i: (0, 0))],
        out_specs=row,
        out_shape=jax.ShapeDtypeStruct((s, d), _F32),
        compiler_params=_params(("arbitrary",), 4 * ts * d * 4 + 4 * ts * d * 4),
        name="final_norm",
    )(x, g.reshape(1, d))


def _rope_kernel(pos_ref, cos_ref, sin_ref):
    half = _HEAD_DIM // 2
    lane = lax.broadcasted_iota(jnp.int32, (1, _HEAD_DIM), 1)
    f = jnp.where(lane < half, lane, lane - half).astype(_F32)
    inv = jnp.exp(f * (-2.0 * math.log(_ROPE_THETA) / _HEAD_DIM))
    ang = pos_ref[...].astype(_F32) * inv
    cos_ref[...] = jnp.cos(ang)
    sin_ref[...] = jnp.where(lane < half, -1.0, 1.0) * jnp.sin(ang)


def _rope_tables(positions):
    s = positions.shape[-1]
    ts = 1024
    tab = pl.BlockSpec((ts, _HEAD_DIM), lambda i: (i, 0))
    return pl.pallas_call(
        _rope_kernel,
        grid=(s // ts,),
        in_specs=[pl.BlockSpec((ts, 1), lambda i: (i, 0))],
        out_specs=[tab, tab],
        out_shape=[jax.ShapeDtypeStruct((s, _HEAD_DIM), _F32)] * 2,
        compiler_params=_params(("arbitrary",), 16 * ts * _HEAD_DIM * 4),
        name="rope_tables",
    )(positions.reshape(s, 1))


_MM_SUB_ROWS = 512


def _mm_kernel(x_ref, w_ref, *rest, epilogue, n_extra, cast_w, side, n_split):
    rest = list(rest)
    extra = [rest.pop(0) for _ in range(n_extra)]
    side_in = rest.pop(0) if side else None
    o_ref = rest.pop(0)
    side_out = rest.pop(0) if side else None
    if cast_w:
        wbf_ref = rest.pop(0)

        @pl.when(pl.program_id(1) == 0)
        def _():
            wbf_ref[...] = w_ref[...].astype(_BF16)

        w = wbf_ref[...]
    else:
        w = w_ref[...]
    sub = x_ref.shape[0] // n_split
    for r in range(n_split):
        rows = slice(r * sub, (r + 1) * sub)
        acc = jnp.dot(x_ref[rows, :].astype(_BF16), w, preferred_element_type=_F32)
        epilogue(acc, extra, o_ref, rows)
    if side:
        side_out[...] = side_in[...].astype(_BF16)


def _ep_cast(acc, extra, o_ref, rows):
    o_ref[rows, :] = acc.astype(o_ref.dtype)


def _ep_silu(acc, extra, o_ref, rows):
    o_ref[rows, :] = _silu(acc).astype(o_ref.dtype)


def _ep_scale(acc, extra, o_ref, rows):
    (scale_ref,) = extra
    o_ref[rows, :] = (acc * scale_ref[...]).astype(o_ref.dtype)


def _ep_silu_gate(acc, extra, o_ref, rows):
    (mixed_ref,) = extra
    o_ref[rows, :] = (_silu(acc) * mixed_ref[rows, :].astype(_F32)).astype(o_ref.dtype)


def _ep_residual(acc, extra, o_ref, rows):
    res_ref, gate_ref = extra
    o_ref[rows, :] = res_ref[rows, :] + gate_ref[...] * acc


def _ep_rotary(acc, extra, o_ref, rows, *, mult):
    cos_ref, sin_ref = extra
    cos = cos_ref[rows, :]
    sin = sin_ref[rows, :]
    for c in range(acc.shape[1] // _HEAD_DIM):
        lanes = slice(c * _HEAD_DIM, (c + 1) * _HEAD_DIM)
        a = acc[:, lanes]
        rot = a * cos + pltpu.roll(a, _HEAD_DIM // 2, axis=1) * sin
        if mult != 1.0:
            rot = rot * mult
        o_ref[rows, lanes] = rot.astype(o_ref.dtype)


def _ep_transpose(acc, extra, o_ref, rows):
    chunk = o_ref.shape[2]
    first = rows.start // chunk
    for r in range(acc.shape[0] // chunk):
        o_ref[first + r] = acc[r * chunk:(r + 1) * chunk, :].T.astype(o_ref.dtype)


def _matmul(x, w, *, k, n, x_map, w_map, w_block, x_block, out_dtype, epilogue,
            extra=(), extra_specs=(), tm, tn, name, out_spec=None, out_shape=None,
            side_cast=None, w_single_buffer=False):
    m = x.shape[-2]
    cast_w = w.dtype != _BF16
    n_i = m // tm
    steps = (n // tn) * n_i
    out_bytes = jnp.dtype(out_dtype).itemsize
    w_bufs = 1 if w_single_buffer else 2
    vmem = (2 * tm * k * x.dtype.itemsize + w_bufs * k * tn * w.dtype.itemsize
            + (k * tn * 2 if cast_w else 0) + 2 * tm * tn * out_bytes
            + 4 * tm * tn * 4 + sum(2 * tm * tn * 4 for _ in extra))
    if out_spec is None:
        out_spec = pl.BlockSpec((tm, tn), lambda j, i: (i, j))
        out_shape = (m, n)
    w_spec = (pl.BlockSpec(w_block, w_map, pipeline_mode=pl.Buffered(1)) if w_single_buffer
              else pl.BlockSpec(w_block, w_map))
    in_specs = [pl.BlockSpec(x_block, x_map), w_spec, *extra_specs]
    out_specs = [out_spec]
    out_shapes = [jax.ShapeDtypeStruct(out_shape, out_dtype)]
    operands = [x, w, *extra]
    if side_cast is not None:
        src, width, col_block = side_cast
        rows = src.shape[0] // steps
        assert rows * steps == src.shape[0] and rows % 16 == 0
        slab = pl.BlockSpec((rows, width), lambda j, i: (j * n_i + i, col_block))
        in_specs.append(slab)
        operands.append(src)
        out_specs.append(pl.BlockSpec((rows, width), lambda j, i: (j * n_i + i, 0)))
        out_shapes.append(jax.ShapeDtypeStruct((src.shape[0], width), _BF16))
        vmem += 2 * rows * width * 6
    res = pl.pallas_call(
        functools.partial(_mm_kernel, epilogue=epilogue, n_extra=len(extra), cast_w=cast_w,
                          side=side_cast is not None, n_split=max(1, tm // _MM_SUB_ROWS)),
        grid=(n // tn, n_i),
        in_specs=in_specs,
        out_specs=out_specs,
        out_shape=out_shapes,
        scratch_shapes=[pltpu.VMEM((k, tn), _BF16)] if cast_w else [],
        compiler_params=_params(("arbitrary", "arbitrary"), vmem),
        name=name,
    )(*operands)
    return res if side_cast is not None else res[0]


def _attn_kernel(q_ref, k_ref, vt_ref, sz_ref, lam_ref, g_ref, o_ref,
                 s0_ref, s1_ref, bias_ref, stat_ref, acc_ref, *, tq, tk, lam_init):
    dh = _HEAD_DIM
    s_refs = (s0_ref, s1_ref)

    @pl.when(pl.program_id(0) == 0)
    def _():
        key_chunk = lax.broadcasted_iota(jnp.int32, (tk, tk), 0) // _CHUNK
        qry_chunk = lax.broadcasted_iota(jnp.int32, (tk, tk), 1) // _CHUNK
        bias_ref[...] = jnp.where(key_chunk <= qry_chunk, 0.0, -jnp.inf)

    lp = lam_ref[...]
    lam = (jnp.exp(jnp.sum(lp[0:1] * lp[1:2], axis=1, keepdims=True))
           - jnp.exp(jnp.sum(lp[2:3] * lp[3:4], axis=1, keepdims=True)) + lam_init)
    g_col = g_ref[...] * (1.0 - lam_init)
    every = slice(0, tq)
    late = slice(tk, tq)

    def tile(i):
        q0 = i * tq
        slot = i % 2
        stat_ref[slot, 0:2, :] = jnp.full((2, tq), -jnp.inf, _F32)
        stat_ref[slot, 2:4, :] = jnp.zeros((2, tq), _F32)
        acc_ref[slot] = jnp.zeros(acc_ref.shape[1:], _F32)

        def scores(chunk, c, qs, diagonal):
            start = chunk * tk if isinstance(chunk, int) else pl.multiple_of(chunk * tk, tk)
            st = lax.dot_general(k_ref[pl.ds(start, tk), c * dh:(c + 1) * dh],
                                 q_ref[q0 + qs.start:q0 + qs.stop, c * dh:(c + 1) * dh],
                                 (((1,), (1,)), ((), ())),
                                 preferred_element_type=_F32)
            if diagonal:
                blk = st[:, :tk] + bias_ref[...]
                st = blk if st.shape[1] == tk else jnp.concatenate([blk, st[:, tk:]], axis=1)
            s_refs[c][:, qs] = st
            stat_ref[slot, 4 + c:5 + c, qs] = jnp.max(st, axis=0, keepdims=True)

        def update(chunk, c, qs):
            st = s_refs[c][:, qs]
            m_prev = stat_ref[slot, c:c + 1, qs]
            m_new = jnp.maximum(m_prev, stat_ref[slot, 4 + c:5 + c, qs])
            alpha = jnp.exp2(m_prev - m_new)
            pt = jnp.exp2(st - m_new)
            stat_ref[slot, 2 + c:3 + c, qs] = (alpha * stat_ref[slot, 2 + c:3 + c, qs]
                                               + jnp.sum(pt, axis=0, keepdims=True))
            stat_ref[slot, c:c + 1, qs] = m_new
            pv = jnp.dot(vt_ref[chunk], pt.astype(_BF16), preferred_element_type=_F32)
            acc_ref[slot, c, :, qs] = alpha * acc_ref[slot, c, :, qs] + pv

        diag_a, diag_b = 2 * i, 2 * i + 1
        scores(diag_b, 0, late, True)
        scores(diag_b, 1, late, True)
        update(diag_b, 0, late)
        scores(diag_a, 0, every, True)
        update(diag_b, 1, late)
        scores(diag_a, 1, every, True)
        update(diag_a, 0, every)
        if i > 0:
            scores(0, 0, every, False)
        update(diag_a, 1, every)

        def pair(p, look_ahead):
            j0, j1 = 2 * p, 2 * p + 1
            scores(j0, 1, every, False)
            update(j0, 0, every)
            scores(j1, 0, every, False)
            update(j0, 1, every)
            scores(j1, 1, every, False)
            update(j1, 0, every)
            if look_ahead:
                scores(j1 + 1, 0, every, False)
            update(j1, 1, every)

        if i > 1:
            def body(p, carry):
                pair(p, True)
                return carry
            lax.fori_loop(0, i - 1, body, 0)
        if i > 0:
            pair(i - 1, False)

        inv1 = 1.0 / stat_ref[slot, 2:3, :]
        inv2 = lam / stat_ref[slot, 3:4, :]
        ot = acc_ref[slot, 0] * inv1 - acc_ref[slot, 1] * inv2
        ms = jnp.mean(ot * ot, axis=0, keepdims=True)
        ont = ot * lax.rsqrt(ms + _EPS) * g_col
        rows = slice(q0, q0 + tq)
        o_ref[rows, :] = (ont.T * sz_ref[rows, :].astype(_F32)).astype(o_ref.dtype)

    for i in range(q_ref.shape[0] // tq):
        tile(i)


_ATTN_KEY_CHUNK = 512
_ATTN_QUERY_TILE = 2 * _ATTN_KEY_CHUNK


def _attention(q, k, vt, sz, lam_params, subln_g, *, lam_init):
    s, width = q.shape
    hw = 2 * _HEAD_DIM
    n_heads = width // hw
    tq, tk = _ATTN_QUERY_TILE, _ATTN_KEY_CHUNK
    head = pl.BlockSpec((s, hw), lambda h: (0, h))
    vmem = 2 * 5 * s * hw * 2 + 4 * hw * tq * 4 + 10 * tk * tq * 4
    return pl.pallas_call(
        functools.partial(_attn_kernel, tq=tq, tk=tk, lam_init=lam_init),
        grid=(n_heads,),
        in_specs=[head, head,
                  pl.BlockSpec((s // tk, hw, tk), lambda h: (0, h, 0)),
                  head,
                  pl.BlockSpec((4, _HEAD_DIM), lambda h: (0, 0)),
                  pl.BlockSpec((hw, 1), lambda h: (0, 0))],
        out_specs=head,
        out_shape=jax.ShapeDtypeStruct((s, width), _BF16),
        scratch_shapes=[pltpu.VMEM((tk, tq), _F32),
                        pltpu.VMEM((tk, tq), _F32),
                        pltpu.VMEM((tk, tk), _F32),
                        pltpu.VMEM((2, 8, tq), _F32),
                        pltpu.VMEM((2, 2, hw, tq), _F32)],
        compiler_params=_params(("arbitrary",), vmem),
        name="diff_attention",
    )(q, k, vt, sz, lam_params, subln_g.reshape(hw, 1))


def kernel(x, c, positions, a_norm_g, a_ada_w, a_ada_b, a_w_in, a_w_group, a_scale, a_w_out,
           kv_norm_g, kv_ada_w, kv_ada_b, w_kv,
           b_norm_g, b_ada_w, b_ada_b, b_w_qg, b_lam, b_subln_g, b_w_out, final_g):
    batch, s, d = x.shape
    n_a = a_norm_g.shape[0]
    n_b = b_norm_g.shape[0]
    assert batch == 1 and n_a == 1 and n_b == 1
    pool_width = a_w_group.shape[1] * a_w_group.shape[2]
    group = a_w_group.shape[2]
    att_width = b_w_out.shape[1]
    assert a_w_group.shape[1] == len(_POOL_WINDOWS)

    h = x[0]
    tm, tn = 1024, 512
    col = lambda j, i: (0, j)

    mod = _ada_mod(c, a_ada_w[0], a_ada_b[0])
    shift, scale, gate = mod[:, :d], mod[:, d:2 * d], mod[:, 2 * d:]
    hn5 = _norm_pool(h, a_norm_g[0], scale, shift)

    tn2 = 2 * tn
    per_group2 = group // tn2
    w_fold = _matmul(a_w_in[0], a_w_group[0], k=group, n=pool_width,
                     x_block=(tm, group), x_map=lambda j, i: (i, j // per_group2),
                     w_block=(None, group, tn2),
                     w_map=lambda j, i: (j // per_group2, 0, j % per_group2),
                     out_dtype=_BF16, epilogue=_ep_cast, tm=tm, tn=tn2, name="fold_group")
    wide_tile = pl.BlockSpec((tm, tn2), lambda j, i: (i, j))
    mixed, w_gate = _matmul(hn5, w_fold, k=d, n=pool_width,
                            x_block=(None, tm, d),
                            x_map=lambda j, i: (1 + j // per_group2, i, 0),
                            w_block=(d, tn2), w_map=col,
                            out_dtype=_BF16, epilogue=_ep_scale,
                            extra=(a_scale[0].reshape(1, pool_width),),
                            extra_specs=(pl.BlockSpec((1, tn2), col),),
                            tm=tm, tn=tn2, name="in_proj_pool",
                            side_cast=(a_w_in[0], pool_width, 1))
    y, w_out_a = _matmul(hn5, w_gate, k=d, n=pool_width,
                         x_block=(None, tm, d), x_map=lambda j, i: (0, i, 0),
                         w_block=(d, tn2), w_map=col,
                         out_dtype=_BF16, epilogue=_ep_silu_gate,
                         extra=(mixed,), extra_specs=(wide_tile,),
                         tm=tm, tn=tn2, name="in_proj_gate",
                         side_cast=(a_w_out[0], d, 0))
    tm3, tn3 = tm // 2, 4 * tn
    res_tile = pl.BlockSpec((tm3, tn3), lambda j, i: (i, j))
    res_vec = pl.BlockSpec((1, tn3), col)
    kv_halves = []
    for kh in range(pool_width // d):
        h, w_half = _matmul(y, w_out_a, k=d, n=d,
                            x_block=(tm3, d), x_map=lambda j, i, kh=kh: (i, kh),
                            w_block=(d, tn3), w_map=lambda j, i, kh=kh: (kh, j),
                            out_dtype=_F32, epilogue=_ep_residual,
                            extra=(h, gate), extra_specs=(res_tile, res_vec),
                            tm=tm3, tn=tn3, name=f"out_proj_a{kh}", w_single_buffer=True,
                            side_cast=(w_kv, att_width, kh))
        kv_halves.append(w_half)
    w_k_bf, w_v_bf = kv_halves

    kv_mod = _ada_mod(c, kv_ada_w, kv_ada_b)
    kv_shift, kv_scale = kv_mod[:, :d], kv_mod[:, d:]
    b_mod = _ada_mod(c, b_ada_w[0], b_ada_b[0])
    b_shift, b_scale, b_gate = b_mod[:, :d], b_mod[:, d:2 * d], b_mod[:, 2 * d:]
    kvn, hnb = _norm2(h, kv_norm_g, kv_scale, kv_shift, b_norm_g[0], b_scale, b_shift)

    cos2, sin2 = _rope_tables(positions)
    rope_specs = (pl.BlockSpec((tm, _HEAD_DIM), lambda j, i: (i, 0)),) * 2
    flat = dict(k=d, n=att_width, x_block=(tm, d), x_map=lambda j, i: (i, 0), tm=tm)
    wide = dict(w_block=(d, tn2), tn=tn2, **flat)
    k_rot, w_q_bf = _matmul(kvn, w_k_bf, w_map=col, out_dtype=_BF16,
                            epilogue=functools.partial(_ep_rotary, mult=1.0),
                            extra=(cos2, sin2), extra_specs=rope_specs, name="k_proj",
                            side_cast=(b_w_qg[0], att_width, 0), **wide)
    tk = _ATTN_KEY_CHUNK
    v_t, w_g_bf = _matmul(kvn, w_v_bf, w_map=col, out_dtype=_BF16,
                          epilogue=_ep_transpose, name="v_proj",
                          out_spec=pl.BlockSpec((tm // tk, tn2, tk), lambda j, i: (i, j, 0)),
                          out_shape=(s // tk, att_width, tk),
                          side_cast=(b_w_qg[0], att_width, 1), **wide)
    q_rot, w_out_b = _matmul(hnb, w_q_bf, w_map=col, out_dtype=_BF16,
                             epilogue=functools.partial(
                                 _ep_rotary, mult=_HEAD_DIM ** -0.5 * math.log2(math.e)),
                             extra=(cos2, sin2), extra_specs=rope_specs, name="q_proj",
                             side_cast=(b_w_out[0], d, 0), **wide)
    szb = _matmul(hnb, w_g_bf, w_map=col, out_dtype=_BF16,
                  epilogue=_ep_silu, name="g_proj", **wide)

    layer = n_a
    lam_init = 0.8 - 0.6 * math.exp(-0.3 * layer)
    yb = _attention(q_rot, k_rot, v_t, szb, b_lam[0], b_subln_g[0], lam_init=lam_init)
    h = _matmul(yb, w_out_b, k=att_width, n=d,
                x_block=(tm3, att_width), x_map=lambda j, i: (i, 0),
                w_block=(att_width, tn3), w_map=col,
                out_dtype=_F32, epilogue=_ep_residual,
                extra=(h, b_gate), extra_specs=(res_tile, res_vec),
                tm=tm3, tn=tn3, name="out_proj_b", w_single_buffer=True)

    return _final_norm(h, final_g)[None]
```
